```python
import math
import jax
import jax.numpy as jnp
from jax import lax
import numpy as np


D_MODEL = 2048
BATCH = 8
SEQ = 2048
DEPTH = 4

N_A_LAYERS = DEPTH // 2
N_B_LAYERS = DEPTH - N_A_LAYERS
NORM_EPS = 1e-6

A_HEADS = 16
A_HEAD_DIM = 128
A_WIDTH = A_HEADS * A_HEAD_DIM
A_CONV = 4
A_CHUNK = 64
A_IN_COLS = 4 * A_WIDTH + 2 * A_HEADS

B_GROUPS = ((128, 1), (512, 4), (2048, 16))
N_GROUPS = len(B_GROUPS)
B_HEADS = 16
B_KV_HEADS = 4
B_HEAD_DIM = 128
B_WIDTH = B_HEADS * B_HEAD_DIM
ATTN_BLOCK = 128

PEER_HEADS = 8
PEER_KEYS = 128
PEER_EXPERTS = PEER_KEYS * PEER_KEYS
PEER_KEY_DIM = 256
PEER_TOPK = 16
PEER_BLOCK = 128

kernel_name = 'hybrid_yoco_deltanet_dilated_peer'


def rmsnorm(x):
    x32 = x.astype(jnp.float32)
    return (x32 * lax.rsqrt(jnp.mean(x32 * x32, axis=-1, keepdims=True) + NORM_EPS)).astype(x.dtype)


def modulate(x, shift, scale):
    return rmsnorm(x) * (1 + scale[:, None, :]) + shift[:, None, :]


def l2norm(t):
    t32 = t.astype(jnp.float32)
    return t32 * lax.rsqrt(jnp.sum(t32 * t32, axis=-1, keepdims=True) + NORM_EPS)


def causal_depthwise_conv(x, w):
    taps, ch = w.shape
    return lax.conv_general_dilated(x, w[:, None, :].astype(x.dtype), window_strides=(1,),
                                    padding=((taps - 1, 0),), dimension_numbers=('NWC', 'WIO', 'NWC'),
                                    feature_group_count=ch)


def gated_delta_rule(q, k, v, g, beta):
    B, S, H, dk = q.shape
    dv = v.shape[-1]
    C = A_CHUNK
    n = S // C

    def chunk(t):
        return t.reshape((B, n, C, H) + t.shape[3:]).transpose((0, 3, 1, 2) + tuple(range(4, t.ndim + 1)))

    qc = chunk(q) * dk ** -0.5
    kc, vc, gc, bc = chunk(k), chunk(v), chunk(g), chunk(beta)
    gam = jnp.cumsum(gc, axis=-1)
    incl = jnp.tril(jnp.ones((C, C), dtype=bool))
    strict = jnp.tril(jnp.ones((C, C), dtype=bool), -1)
    decay = jnp.exp(jnp.where(incl, gam[..., :, None] - gam[..., None, :], -jnp.inf))
    a_mat = jnp.where(strict, jnp.einsum('bhncd,bhnmd->bhncm', kc, kc) * decay * bc[..., :, None], 0.0)
    rhs = jnp.concatenate([vc * bc[..., None], kc * (bc * jnp.exp(gam))[..., None]], axis=-1)
    sol = lax.linalg.triangular_solve(a_mat + jnp.eye(C, dtype=a_mat.dtype), rhs, left_side=True,
                                      lower=True, unit_diagonal=True)
    u, w = sol[..., :dv], sol[..., dv:]
    qk = jnp.einsum('bhncd,bhnmd->bhncm', qc, kc) * decay
    q_dec = qc * jnp.exp(gam)[..., None]
    k_dec = kc * jnp.exp(gam[..., -1:] - gam)[..., None]
    chunk_decay = jnp.exp(gam[..., -1])

    def step(state, inp):
        qk_i, qd_i, kd_i, u_i, w_i, cd_i = inp
        v_new = u_i - jnp.einsum('bhcd,bhde->bhce', w_i, state)
        o_i = jnp.einsum('bhcd,bhde->bhce', qd_i, state) + jnp.einsum('bhcm,bhme->bhce', qk_i, v_new)
        state = state * cd_i[..., None, None] + jnp.einsum('bhcd,bhce->bhde', kd_i, v_new)
        return state, o_i

    xs = tuple(jnp.moveaxis(t, 2, 0) for t in (qk, q_dec, k_dec, u, w, chunk_decay))
    _, o = lax.scan(step, jnp.zeros((B, H, dk, dv), q.dtype), xs)
    return o.transpose(1, 0, 3, 2, 4).reshape(B, S, H, dv)


def gated_deltanet(h, in_w, conv_w, a_log, dt_bias, norm_g, out_w):
    B, S, _ = h.shape
    proj = h @ in_w
    qkv = jax.nn.silu(causal_depthwise_conv(proj[..., :3 * A_WIDTH], conv_w))
    qkv = qkv.reshape(B, S, 3, A_HEADS, A_HEAD_DIM)
    z = proj[..., 3 * A_WIDTH:4 * A_WIDTH].reshape(B, S, A_HEADS, A_HEAD_DIM).astype(jnp.float32)
    b_raw = proj[..., 4 * A_WIDTH:4 * A_WIDTH + A_HEADS].astype(jnp.float32)
    a_raw = proj[..., 4 * A_WIDTH + A_HEADS:].astype(jnp.float32)
    beta = jax.nn.sigmoid(b_raw)
    g = -jnp.exp(a_log.astype(jnp.float32)) * jax.nn.softplus(a_raw + dt_bias.astype(jnp.float32))
    o = gated_delta_rule(l2norm(qkv[:, :, 0]), l2norm(qkv[:, :, 1]), qkv[:, :, 2].astype(jnp.float32), g, beta)
    o = o * lax.rsqrt(jnp.mean(o * o, axis=-1, keepdims=True) + NORM_EPS) * norm_g.astype(jnp.float32) * jax.nn.silu(z)
    return o.astype(h.dtype).reshape(B, S, A_WIDTH) @ out_w


def alibi_slopes():
    n = N_GROUPS * B_HEADS
    return (2.0 ** (-8.0 * jnp.arange(1, n + 1, dtype=jnp.float32) / n)).reshape(N_GROUPS, B_HEADS)


def dilated_branch(q, k, v, window, dil, slopes):
    B, S, H, dh = q.shape
    hkv = k.shape[2]
    grp = H // hkv
    span = dil * ATTN_BLOCK
    s_pad = -(-S // span) * span
    pad = ((0, 0), (0, s_pad - S), (0, 0), (0, 0))
    q, k, v = jnp.pad(q, pad), jnp.pad(k, pad), jnp.pad(v, pad)
    L = s_pad // dil
    nb = L // ATTN_BLOCK

    def strided(t):
        return t.reshape(B, L, dil, t.shape[2], dh).transpose(0, 2, 1, 3, 4).reshape(B, dil, nb, ATTN_BLOCK, t.shape[2], dh)

    def with_prev(t):
        prev = jnp.pad(t, ((0, 0), (0, 0), (1, 0), (0, 0), (0, 0), (0, 0)))[:, :, :-1]
        return jnp.concatenate([prev, t], axis=3)

    qb = strided(q).reshape(B, dil, nb, ATTN_BLOCK, hkv, grp, dh)
    kk, vv = with_prev(strided(k)), with_prev(strided(v))
    scores = jnp.einsum('brnqhgd,brnkhd->brnhgqk', qb, kk).astype(jnp.float32) * dh ** -0.5
    qi = jnp.arange(ATTN_BLOCK) + ATTN_BLOCK
    kj = jnp.arange(2 * ATTN_BLOCK)
    dist = qi[:, None] - kj[None, :]
    span_ok = (dist >= 0) & (dist <= window // dil)
    first = (jnp.arange(nb)[:, None, None] > 0) | (kj[None, None, :] >= ATTN_BLOCK)
    mask = span_ok[None] & first
    bias = -slopes.reshape(hkv, grp)[:, :, None, None] * (dil * dist).astype(jnp.float32)[None, None]
    scores = jnp.where(mask[:, None, None], scores + bias, -jnp.inf)
    lse = jax.nn.logsumexp(scores, axis=-1)
    p = jnp.exp(scores - lse[..., None])
    o = jnp.einsum('brnhgqk,brnkhd->brnqhgd', p.astype(v.dtype), vv)

    def unstrided(t):
        tail = t.shape[4:]
        return t.reshape((B, dil, L) + tail).swapaxes(1, 2).reshape((B, s_pad) + tail)[:, :S]

    o = unstrided(o).reshape(B, S, H, dh)
    lse = unstrided(lse.transpose(0, 1, 2, 5, 3, 4)).reshape(B, S, H)
    return o, lse


def dilated_attention(h, k_sh, v_sh, q_w, out_w):
    B, S, _ = h.shape
    q = (h @ q_w).reshape(B, S, N_GROUPS, B_HEADS, B_HEAD_DIM)
    slopes = alibi_slopes()
    outs, lses = [], []
    for gi, (window, dil) in enumerate(B_GROUPS):
        o_g, l_g = dilated_branch(q[:, :, gi], k_sh[:, :, gi], v_sh[:, :, gi], window, dil, slopes[gi])
        outs.append(o_g)
        lses.append(l_g)
    wts = jax.nn.softmax(jnp.stack(lses), axis=0)
    o = jnp.einsum('gbsh,gbshd->bshd', wts, jnp.stack(outs).astype(jnp.float32))
    return o.astype(h.dtype).reshape(B, S, B_WIDTH) @ out_w


def shared_kv(x, cs, kv_mod_w, kv_mod_b, kv_w):
    B, S, _ = x.shape
    shift, scale = jnp.split(cs @ kv_mod_w + kv_mod_b, 2, axis=-1)
    kv = (modulate(x, shift, scale) @ kv_w).reshape(B, S, 2, N_GROUPS, B_KV_HEADS, B_HEAD_DIM)
    return kv[:, :, 0], kv[:, :, 1]


def peer_ffn(h, q_w, subkeys, u_tab, v_tab):
    B, S, D = h.shape

    def block(xb):
        T = xb.shape[0]
        q = (xb @ q_w).reshape(T, PEER_HEADS, 2, PEER_KEY_DIM // 2)
        sub = jnp.einsum('thpd,pkd->thpk', q, subkeys).astype(jnp.float32)
        sv, si = lax.top_k(sub, PEER_TOPK)
        cand_s = (sv[:, :, 0, :, None] + sv[:, :, 1, None, :]).reshape(T, PEER_HEADS, -1)
        cand_i = (si[:, :, 0, :, None] * PEER_KEYS + si[:, :, 1, None, :]).reshape(T, PEER_HEADS, -1)
        top_s, pos = lax.top_k(cand_s, PEER_TOPK)
        idx = jnp.take_along_axis(cand_i, pos, axis=-1).reshape(T, -1)
        gate = jax.nn.softmax(top_s, axis=-1).reshape(T, -1)
        u = jnp.take(u_tab, idx, axis=0)
        v = jnp.take(v_tab, idx, axis=0)
        act = jax.nn.gelu(jnp.einsum('td,ted->te', xb, u).astype(jnp.float32))
        return jnp.einsum('te,ted->td', (gate * act).astype(xb.dtype), v)

    y = lax.map(block, h.reshape(B * S // PEER_BLOCK, PEER_BLOCK, D))
    return y.reshape(B, S, D)


def setup_inputs(seed: int = 0) -> dict:
    key = jax.random.key(seed)
    ks = jax.random.split(key, 20)
    f32 = jnp.float32

    def nrm(k, shape, scale):
        return jax.random.normal(k, shape, f32) * scale

    D = D_MODEL
    dt = jnp.exp(jax.random.uniform(ks[7], (N_A_LAYERS, A_HEADS), f32, math.log(1e-3), math.log(1e-1)))
    return {
        'x': nrm(ks[0], (BATCH, SEQ, D), 1.0),
        'c': nrm(ks[1], (BATCH, D), 1.0),
        'mod_w': nrm(ks[2], (DEPTH, D, 6 * D), 0.5 * D ** -0.5),
        'mod_b': nrm(ks[3], (DEPTH, 6 * D), 0.02),
        'a_in_w': nrm(ks[4], (N_A_LAYERS, D, A_IN_COLS), D ** -0.5),
        'a_conv_w': nrm(ks[5], (N_A_LAYERS, A_CONV, 3 * A_WIDTH), A_CONV ** -0.5),
        'a_log': jnp.log(jax.random.uniform(ks[6], (N_A_LAYERS, A_HEADS), f32, 1.0, 16.0)),
        'a_dt_bias': dt + jnp.log(-jnp.expm1(-dt)),
        'a_norm_g': 1.0 + nrm(ks[8], (N_A_LAYERS, A_HEAD_DIM), 0.1),
        'a_out_w': nrm(ks[9], (N_A_LAYERS, A_WIDTH, D), A_WIDTH ** -0.5),
        'kv_mod_w': nrm(ks[10], (D, 2 * D), 0.5 * D ** -0.5),
        'kv_mod_b': nrm(ks[11], (2 * D,), 0.02),
        'kv_w': nrm(ks[12], (D, 2 * N_GROUPS * B_KV_HEADS * B_HEAD_DIM), D ** -0.5),
        'b_q_w': nrm(ks[13], (N_B_LAYERS, D, N_GROUPS * B_WIDTH), D ** -0.5),
        'b_out_w': nrm(ks[14], (N_B_LAYERS, B_WIDTH, D), B_WIDTH ** -0.5),
        'peer_q_w': nrm(ks[15], (DEPTH, D, PEER_HEADS * PEER_KEY_DIM), D ** -0.5),
        'peer_subkeys': nrm(ks[16], (DEPTH, 2, PEER_KEYS, PEER_KEY_DIM // 2), (PEER_KEY_DIM // 2) ** -0.5),
        'peer_u': nrm(ks[17], (DEPTH, PEER_EXPERTS, D), D ** -0.5),
        'peer_v': nrm(ks[18], (DEPTH, PEER_EXPERTS, D), 0.5),
        'final_g': 1.0 + nrm(ks[19], (D,), 0.1),
    }


def reference(x, c, mod_w, mod_b, a_in_w, a_conv_w, a_log, a_dt_bias, a_norm_g, a_out_w,
              kv_mod_w, kv_mod_b, kv_w, b_q_w, b_out_w, peer_q_w, peer_subkeys, peer_u, peer_v, final_g):
    cs = jax.nn.silu(c)
    k_sh, v_sh = None, None
    for layer in range(DEPTH):
        mod = cs @ mod_w[layer] + mod_b[layer]
        sh_m, sc_m, g_m, sh_f, sc_f, g_f = jnp.split(mod, 6, axis=-1)
        if layer < N_A_LAYERS:
            mix = gated_deltanet(modulate(x, sh_m, sc_m), a_in_w[layer], a_conv_w[layer], a_log[layer],
                                 a_dt_bias[layer], a_norm_g[layer], a_out_w[layer])
        else:
            if layer == N_A_LAYERS:
                k_sh, v_sh = shared_kv(x, cs, kv_mod_w, kv_mod_b, kv_w)
            j = layer - N_A_LAYERS
            mix = dilated_attention(modulate(x, sh_m, sc_m), k_sh, v_sh, b_q_w[j], b_out_w[j])
        x = x + g_m[:, None, :] * mix
        x = x + g_f[:, None, :] * peer_ffn(modulate(x, sh_f, sc_f), peer_q_w[layer], peer_subkeys[layer],
                                           peer_u[layer], peer_v[layer])
    return rmsnorm(x) * final_g
```

```python
import functools

import jax
import jax.numpy as jnp
from jax import lax
from jax.experimental import pallas as pl
from jax.experimental.pallas import tpu as pltpu

F32 = jnp.float32
BF16 = jnp.bfloat16
HI = lax.Precision.HIGHEST

NORM_EPS = 1e-6
LANES = 128
SUBLANES = 8

A_HEADS = 16
HEAD_DIM = 128
A_CHUNK = 64
A_CONV = 4
B_GROUPS = ((128, 1), (512, 4), (2048, 16))
B_HEADS = 16
B_KV_HEADS = 4
ATTN_BLOCK = 128
PEER_HEADS = 8
PEER_KEYS = 128
PEER_TOPK = 16
PEER_SEL = PEER_HEADS * PEER_TOPK

VMEM_LIMIT = 56 * 1024 * 1024


def _params(sem):
    return pltpu.CompilerParams(dimension_semantics=sem, vmem_limit_bytes=VMEM_LIMIT)


def _tile(n, pref):
    t = pref
    while t > LANES and n % t:
        t //= 2
    assert n % t == 0, (n, pref)
    return t


def _rms_rows(x):
    return x * lax.rsqrt(jnp.mean(x * x, axis=-1, keepdims=True) + NORM_EPS)


def _dot_t(a, b, precision=None):
    return lax.dot_general(a, b, (((1,), (1,)), ((), ())), precision=precision,
                           preferred_element_type=F32)


def _mod_kernel(c_ref, w_ref, b_ref, o_ref):
    c = c_ref[...]
    cs = (c * jax.nn.sigmoid(c)).astype(BF16)
    o_ref[0] = jnp.dot(cs, w_ref[0].astype(BF16), preferred_element_type=F32) + b_ref[0]


def _mod_matmul(c, w, b, tn=1024):
    L, D, N = w.shape
    B = c.shape[0]
    tn = _tile(N, tn)
    return pl.pallas_call(
        _mod_kernel,
        grid=(L, N // tn),
        in_specs=[pl.BlockSpec((B, D), lambda l, j: (0, 0)),
                  pl.BlockSpec((1, D, tn), lambda l, j: (l, 0, j)),
                  pl.BlockSpec((1, 1, tn), lambda l, j: (l, 0, j))],
        out_specs=pl.BlockSpec((1, B, tn), lambda l, j: (l, 0, j)),
        out_shape=jax.ShapeDtypeStruct((L, B, N), F32),
        compiler_params=_params(("arbitrary", "arbitrary")),
        name="adaln_mod",
    )(c, w, b.reshape(L, 1, N))


def _mm_kernel(*refs, modulate, residual):
    it = iter(refs)
    a_ref = next(it)
    if modulate:
        sh_ref, sc_ref = next(it), next(it)
    w_ref = next(it)
    if residual:
        res_ref, g_ref = next(it), next(it)
    o_ref = next(it)
    a_bf = next(it)

    @pl.when(pl.program_id(1) == 0)
    def _():
        a = a_ref[...]
        if modulate:
            a = _rms_rows(a) * (1.0 + sc_ref[0]) + sh_ref[0]
        a_bf[...] = a.astype(BF16)

    acc = jnp.dot(a_bf[...], w_ref[...].astype(BF16), preferred_element_type=F32)
    if residual:
        acc = res_ref[...] + g_ref[0] * acc
    o_ref[...] = acc


def _matmul(a, w, *, n_out=None, shift=None, scale=None, res=None, gate=None, seq=None,
            tm=1024, tn=512, name="matmul"):
    M, K = a.shape
    N = w.shape[1] if n_out is None else n_out
    tn = _tile(N, tn)
    tm = _tile(seq, tm)
    modulate = shift is not None
    residual = res is not None
    bpb = seq // tm
    args = [a]
    specs = [pl.BlockSpec((tm, K), lambda i, j: (i, 0))]
    if modulate:
        B = shift.shape[0]
        args += [shift.reshape(B, 1, K), scale.reshape(B, 1, K)]
        specs += [pl.BlockSpec((1, 1, K), lambda i, j: (i // bpb, 0, 0))] * 2
    args.append(w)
    specs.append(pl.BlockSpec((K, tn), lambda i, j: (0, j)))
    if residual:
        B = gate.shape[0]
        args += [res, gate.reshape(B, 1, N)]
        specs += [pl.BlockSpec((tm, tn), lambda i, j: (i, j)),
                  pl.BlockSpec((1, 1, tn), lambda i, j: (i // bpb, 0, j))]
    return pl.pallas_call(
        functools.partial(_mm_kernel, modulate=modulate, residual=residual),
        grid=(M // tm, N // tn),
        in_specs=specs,
        out_specs=pl.BlockSpec((tm, tn), lambda i, j: (i, j)),
        out_shape=jax.ShapeDtypeStruct((M, N), F32),
        scratch_shapes=[pltpu.VMEM((tm, K), BF16)],
        compiler_params=_params(("arbitrary", "arbitrary")),
        name=name,
    )(*args)


def _gdn_kernel(q_ref, k_ref, v_ref, z_ref, cwq_ref, cwk_ref, cwv_ref, gr_ref, alog_ref, dtb_ref,
                ng_ref, o_ref, qs_s, k_s, g_s, beta_s, u_s, w_s, qd_s, kd_s, qk_s, cd_s):
    h = pl.program_id(1)
    S = q_ref.shape[1]
    C = A_CHUNK
    n_chunks = S // C
    row = lax.broadcasted_iota(jnp.int32, (S, HEAD_DIM), 0)

    def conv_silu(x, w):
        y = x * w[A_CONV - 1:A_CONV, :]
        for back in range(1, A_CONV):
            xs = jnp.where(row >= back, pltpu.roll(x, back, axis=0), 0.0)
            y = y + xs * w[A_CONV - 1 - back:A_CONV - back, :]
        return y * jax.nn.sigmoid(y)

    def l2n(t):
        return t * lax.rsqrt(jnp.sum(t * t, axis=-1, keepdims=True) + NORM_EPS)

    qs_s[...] = l2n(conv_silu(q_ref[0], cwq_ref[...])) * (HEAD_DIM ** -0.5)
    k_s[...] = l2n(conv_silu(k_ref[0], cwk_ref[...]))
    vv = conv_silu(v_ref[0], cwv_ref[...])

    lane = lax.broadcasted_iota(jnp.int32, (1, LANES), 1)
    oh_b = (lane == h).astype(F32)
    oh_a = (lane == h + A_HEADS).astype(F32)
    gr = gr_ref[0]
    b_raw = jnp.sum(gr * oh_b, axis=-1, keepdims=True)
    a_raw = jnp.sum(gr * oh_a, axis=-1, keepdims=True)
    a_log = jnp.sum(alog_ref[...] * oh_b, axis=-1, keepdims=True)
    dtb = jnp.sum(dtb_ref[...] * oh_b, axis=-1, keepdims=True)
    sp_in = a_raw + dtb
    softplus = jnp.maximum(sp_in, 0.0) + jnp.log1p(jnp.exp(-jnp.abs(sp_in)))
    g_s[...] = -jnp.exp(a_log) * softplus
    beta = jax.nn.sigmoid(b_raw)
    beta_s[...] = beta
    u_s[...] = vv * beta

    ri = lax.broadcasted_iota(jnp.int32, (C, C), 0)
    ci = lax.broadcasted_iota(jnp.int32, (C, C), 1)
    incl = ri >= ci
    strict = ri > ci
    lower = incl.astype(F32)
    upper = (ri <= ci).astype(F32)
    ones = jnp.ones((C, C), F32)
    eye = (ri == ci).astype(F32)

    def prep(c, carry):
        r0 = pl.multiple_of(c * C, C)
        rows = pl.ds(r0, C)
        g_c = g_s[rows, :]
        b_c = beta_s[rows, :]
        k_c = k_s[rows, :]
        q_c = qs_s[rows, :]
        gcol = jnp.dot(lower, jnp.broadcast_to(g_c, (C, C)), precision=HI,
                       preferred_element_type=F32)
        grow = jnp.dot(ones, g_c * upper, precision=HI, preferred_element_type=F32)
        decay = jnp.exp(jnp.where(incl, gcol - grow, -jnp.inf))
        gam = gcol[:, 0:1]
        gam_last = grow[:, C - 1:C]
        k_bf = k_c.astype(BF16)
        kk = _dot_t(k_bf, k_bf)
        a_mat = jnp.where(strict, kk * decay * b_c, 0.0)
        inv = eye - a_mat
        pw = a_mat
        for _ in range(5):
            pw = jnp.dot(pw, pw, precision=HI, preferred_element_type=F32)
            inv = inv + jnp.dot(inv, pw, precision=HI, preferred_element_type=F32)
        egam = jnp.exp(gam)
        u_s[rows, :] = jnp.dot(inv, u_s[rows, :], precision=HI, preferred_element_type=F32)
        w_s[rows, :] = jnp.dot(inv, k_c * (b_c * egam), precision=HI, preferred_element_type=F32)
        qk_s[rows, :] = _dot_t(q_c.astype(BF16), k_bf) * decay
        qd_s[rows, :] = q_c * egam
        kd_s[rows, :] = k_c * jnp.exp(gam_last - gam)
        cd_s[pl.ds(pl.multiple_of(c * SUBLANES, SUBLANES), SUBLANES), :] = jnp.broadcast_to(
            jnp.exp(gam_last[0:SUBLANES, :]), (SUBLANES, HEAD_DIM))
        return carry

    lax.fori_loop(0, n_chunks, prep, 0)

    ng = ng_ref[...]

    def scan(c, state):
        r0 = pl.multiple_of(c * C, C)
        rows = pl.ds(r0, C)
        st_bf = state.astype(BF16)
        v_new = u_s[rows, :] - jnp.dot(w_s[rows, :].astype(BF16), st_bf, preferred_element_type=F32)
        vn_bf = v_new.astype(BF16)
        o_c = (jnp.dot(qd_s[rows, :].astype(BF16), st_bf, preferred_element_type=F32)
               + jnp.dot(qk_s[rows, :].astype(BF16), vn_bf, preferred_element_type=F32))
        cd = cd_s[pl.ds(pl.multiple_of(c * SUBLANES, SUBLANES), 1), :]
        state = state * cd + lax.dot_general(kd_s[rows, :].astype(BF16), vn_bf,
                                             (((0,), (0,)), ((), ())), preferred_element_type=F32)
        z_c = z_ref[0, rows, :]
        o_ref[0, rows, :] = _rms_rows(o_c) * ng * (z_c * jax.nn.sigmoid(z_c))
        return state

    lax.fori_loop(0, n_chunks, scan, jnp.zeros((HEAD_DIM, HEAD_DIM), F32))


def _gated_deltanet_core(proj, gates_raw, conv_w, a_log, dt_bias, norm_g, B, S):
    H = A_HEADS
    C = A_CHUNK

    def col(off):
        return pl.BlockSpec((1, S, HEAD_DIM), lambda b, h: (b, 0, off + h))

    def cw(off):
        return pl.BlockSpec((A_CONV, HEAD_DIM), lambda b, h: (0, off + h))

    row128 = pl.BlockSpec((1, LANES), lambda b, h: (0, 0))
    pad = lambda t: jnp.pad(t.reshape(1, -1), ((0, 0), (0, LANES - t.shape[-1])))
    return pl.pallas_call(
        _gdn_kernel,
        grid=(B, H),
        in_specs=[col(0), col(H), col(2 * H), col(3 * H), cw(0), cw(H), cw(2 * H),
                  pl.BlockSpec((1, S, LANES), lambda b, h: (b, 0, 0)),
                  row128, row128, row128],
        out_specs=pl.BlockSpec((1, S, HEAD_DIM), lambda b, h: (b, 0, h)),
        out_shape=jax.ShapeDtypeStruct((B, S, H * HEAD_DIM), F32),
        scratch_shapes=[pltpu.VMEM((S, HEAD_DIM), F32), pltpu.VMEM((S, HEAD_DIM), F32),
                        pltpu.VMEM((S, 1), F32), pltpu.VMEM((S, 1), F32),
                        pltpu.VMEM((S, HEAD_DIM), F32), pltpu.VMEM((S, HEAD_DIM), F32),
                        pltpu.VMEM((S, HEAD_DIM), F32), pltpu.VMEM((S, HEAD_DIM), F32),
                        pltpu.VMEM((S, C), F32),
                        pltpu.VMEM((S // C * SUBLANES, HEAD_DIM), F32)],
        compiler_params=_params(("arbitrary", "arbitrary")),
        name="gated_deltanet",
    )(proj, proj, proj, proj, conv_w, conv_w, conv_w, gates_raw, pad(a_log), pad(dt_bias),
      norm_g.reshape(1, HEAD_DIM))


def _attn_kernel(q_ref, kp_ref, kc_ref, vp_ref, vc_ref, o_ref, lse_ref, *, dil, slopes):
    n = pl.program_id(2)
    T = ATTN_BLOCK
    qi = lax.broadcasted_iota(jnp.int32, (T, 2 * T), 0) + T
    kj = lax.broadcasted_iota(jnp.int32, (T, 2 * T), 1)
    dist = qi - kj
    mask = (dist >= 0) & (dist <= T) & ((kj >= T) | (n > 0))
    distf = (dil * dist).astype(F32)
    lane = lax.broadcasted_iota(jnp.int32, (T, LANES), 1)
    lse_all = jnp.zeros((T, LANES), F32)
    grp = B_HEADS // B_KV_HEADS
    for kvh in range(B_KV_HEADS):
        cs = slice(kvh * HEAD_DIM, (kvh + 1) * HEAD_DIM)
        kcat = jnp.concatenate([kp_ref[0, :, cs], kc_ref[0, :, cs]], axis=0).astype(BF16)
        vcat = jnp.concatenate([vp_ref[0, :, cs], vc_ref[0, :, cs]], axis=0).astype(BF16)
        for gq in range(grp):
            hh = kvh * grp + gq
            hs = slice(hh * HEAD_DIM, (hh + 1) * HEAD_DIM)
            s = _dot_t(q_ref[0, :, hs].astype(BF16), kcat) * (HEAD_DIM ** -0.5)
            s = jnp.where(mask, s - slopes[hh] * distf, -jnp.inf)
            m = jnp.max(s, axis=-1, keepdims=True)
            e = jnp.exp(s - m)
            l = jnp.sum(e, axis=-1, keepdims=True)
            p = e / l
            o_ref[0, :, hs] = jnp.dot(p.astype(BF16), vcat, preferred_element_type=F32)
            lse_all = jnp.where(lane == hh, m + jnp.log(l), lse_all)
    lse_ref[0] = lse_all


def _dilated_group(q, kv, gi, B, S):
    G = len(B_GROUPS)
    window, dil = B_GROUPS[gi]
    assert window // dil == ATTN_BLOCK and S % (dil * ATTN_BLOCK) == 0
    L = S // dil
    nb = L // ATTN_BLOCK
    W = B_HEADS * HEAD_DIM
    KVW = B_KV_HEADS * HEAD_DIM
    n_slopes = G * B_HEADS
    slopes = tuple(2.0 ** (-8.0 * (gi * B_HEADS + hh + 1) / n_slopes) for hh in range(B_HEADS))
    qv = q.reshape(B, L, dil * G * W)
    kvv = kv.reshape(B, L, dil * 2 * G * KVW)
    T = ATTN_BLOCK

    def kspec(off, prev):
        if prev:
            return pl.BlockSpec((1, T, KVW), lambda b, r, n: (b, jnp.maximum(n - 1, 0), r * 2 * G + off))
        return pl.BlockSpec((1, T, KVW), lambda b, r, n: (b, n, r * 2 * G + off))

    o, lse = pl.pallas_call(
        functools.partial(_attn_kernel, dil=dil, slopes=slopes),
        grid=(B, dil, nb),
        in_specs=[pl.BlockSpec((1, T, W), lambda b, r, n: (b, n, r * G + gi)),
                  kspec(gi, True), kspec(gi, False), kspec(G + gi, True), kspec(G + gi, False)],
        out_specs=[pl.BlockSpec((1, T, W), lambda b, r, n: (b, n, r)),
                   pl.BlockSpec((1, T, LANES), lambda b, r, n: (b, n, r))],
        out_shape=[jax.ShapeDtypeStruct((B, L, dil * W), F32),
                   jax.ShapeDtypeStruct((B, L, dil * LANES), F32)],
        compiler_params=_params(("arbitrary", "arbitrary", "arbitrary")),
        name=f"dilated_attn_g{gi}",
    )(qv, kvv, kvv, kvv, kvv)
    return o.reshape(B * S, W), lse.reshape(B * S, LANES)


def _combine_kernel(o0, o1, o2, l0, l1, l2, out_ref):
    ls = [l0[...], l1[...], l2[...]]
    m = jnp.maximum(jnp.maximum(ls[0], ls[1]), ls[2])
    es = [jnp.exp(l - m) for l in ls]
    inv = 1.0 / (es[0] + es[1] + es[2])
    ws = [e * inv for e in es]
    for hh in range(B_HEADS):
        hs = slice(hh * HEAD_DIM, (hh + 1) * HEAD_DIM)
        out_ref[:, hs] = (ws[0][:, hh:hh + 1] * o0[:, hs] + ws[1][:, hh:hh + 1] * o1[:, hs]
                          + ws[2][:, hh:hh + 1] * o2[:, hs])


def _combine_groups(outs, lses, tm=512):
    M, W = outs[0].shape
    tm = _tile(M, tm)
    ospec = pl.BlockSpec((tm, W), lambda i: (i, 0))
    lspec = pl.BlockSpec((tm, LANES), lambda i: (i, 0))
    return pl.pallas_call(
        _combine_kernel,
        grid=(M // tm,),
        in_specs=[ospec] * 3 + [lspec] * 3,
        out_specs=ospec,
        out_shape=jax.ShapeDtypeStruct((M, W), F32),
        compiler_params=_params(("arbitrary",)),
        name="attn_combine",
    )(*outs, *lses)


def _topk_rows(s, pos, k):
    big = float(s.shape[0])
    vals, ids = [], []
    for _ in range(k):
        m = jnp.max(s, axis=0, keepdims=True)
        i = jnp.min(jnp.where(s == m, pos, big), axis=0, keepdims=True)
        vals.append(m)
        ids.append(i)
        s = jnp.where(pos == i, -jnp.inf, s)
    return jnp.concatenate(vals, axis=0), jnp.concatenate(ids, axis=0)


def _peer_route_kernel(q_ref, sk_ref, idx_ref, gate_ref):
    T = q_ref.shape[0]
    K = PEER_TOPK
    kpos = lax.broadcasted_iota(jnp.int32, (PEER_KEYS, T), 0).astype(F32)
    cpos = lax.broadcasted_iota(jnp.int32, (K * K, T), 0).astype(F32)
    sk = [sk_ref[0].astype(BF16), sk_ref[1].astype(BF16)]
    idx_rows, gate_rows = [], []
    for hd in range(PEER_HEADS):
        sv, si = [], []
        for p in range(2):
            c0 = (hd * 2 + p) * HEAD_DIM
            sub = _dot_t(sk[p], q_ref[:, c0:c0 + HEAD_DIM].astype(BF16))
            v, i = _topk_rows(sub, kpos, K)
            sv.append(v)
            si.append(i)
        cand_s = jnp.concatenate([sv[0][a:a + 1, :] + sv[1] for a in range(K)], axis=0)
        cand_i = jnp.concatenate([si[0][a:a + 1, :] * float(PEER_KEYS) + si[1] for a in range(K)],
                                 axis=0)
        top_s, pos = _topk_rows(cand_s, cpos, K)
        e_rows = [jnp.sum(jnp.where(cpos == pos[r:r + 1, :], cand_i, 0.0), axis=0, keepdims=True)
                  for r in range(K)]
        ex = jnp.exp(top_s - top_s[0:1, :])
        gate_rows.append(ex / jnp.sum(ex, axis=0, keepdims=True))
        idx_rows.append(jnp.concatenate(e_rows, axis=0))
    idx_ref[...] = jnp.concatenate(idx_rows, axis=0).T.astype(jnp.int32)
    gate_ref[...] = jnp.concatenate(gate_rows, axis=0).T


def _peer_route(q, subkeys, tt=128):
    M, W = q.shape
    return pl.pallas_call(
        _peer_route_kernel,
        grid=(M // tt,),
        in_specs=[pl.BlockSpec((tt, W), lambda i: (i, 0)),
                  pl.BlockSpec(subkeys.shape, lambda i: (0, 0, 0))],
        out_specs=[pl.BlockSpec((tt, PEER_SEL), lambda i: (i, 0))] * 2,
        out_shape=[jax.ShapeDtypeStruct((M, PEER_SEL), jnp.int32),
                   jax.ShapeDtypeStruct((M, PEER_SEL), F32)],
        compiler_params=_params(("arbitrary",)),
        name="peer_route",
    )(q, subkeys)


def _peer_expert_kernel(idxc_ref, idxn_ref, gate_ref, x_ref, sh_ref, sc_ref, gf_ref, tab_ref,
                        o_ref, buf, sem, *, tb):
    i = pl.program_id(0)
    n = pl.num_programs(0)
    slot = lax.rem(i, 2)
    D = x_ref.shape[1]
    rows = tb * PEER_SEL

    def row_copy(e, r, s):
        return pltpu.make_async_copy(tab_ref.at[pl.ds(e, 1), :], buf.at[s, pl.ds(r, 1), :],
                                     sem.at[s])

    def issue(idx_ref, s):
        def body(r8, carry):
            for k in range(SUBLANES):
                r = r8 * SUBLANES + k
                row_copy(idx_ref[0, 0, r], r, s).start()
            return carry
        lax.fori_loop(0, rows // SUBLANES, body, 0)

    @pl.when(i == 0)
    def _():
        issue(idxc_ref, 0)

    @pl.when(i + 1 < n)
    def _():
        issue(idxn_ref, 1 - slot)

    pltpu.make_async_copy(buf.at[slot], buf.at[slot], sem.at[slot]).wait()

    x = x_ref[...]
    hmod = _rms_rows(x) * (1.0 + sc_ref[0]) + sh_ref[0]
    gf = gf_ref[0]
    ri = lax.broadcasted_iota(jnp.int32, (PEER_SEL, PEER_SEL), 0)
    ci = lax.broadcasted_iota(jnp.int32, (PEER_SEL, PEER_SEL), 1)
    diag = ri == ci
    for t in range(tb):
        rs = slice(t * PEER_SEL, (t + 1) * PEER_SEL)
        u = buf[slot, rs, 0:D]
        act = jnp.sum(u * hmod[t:t + 1, :], axis=-1, keepdims=True)
        gate_col = jnp.sum(jnp.where(diag, gate_ref[t:t + 1, :], 0.0), axis=-1, keepdims=True)
        coef = gate_col * jax.nn.gelu(act)
        y = jnp.sum(coef * buf[slot, rs, D:2 * D], axis=0, keepdims=True)
        o_ref[t:t + 1, :] = x[t:t + 1, :] + gf * y


def _peer_experts(x, idx, gate, shift, scale, gate_f, table, seq, tb=8):
    M, D = x.shape
    B = shift.shape[0]
    nblk = M // tb
    bpb = seq // tb
    idx3 = idx.reshape(nblk, 1, tb * PEER_SEL)
    vec = pl.BlockSpec((1, 1, D), lambda i: (i // bpb, 0, 0))
    return pl.pallas_call(
        functools.partial(_peer_expert_kernel, tb=tb),
        grid=(nblk,),
        in_specs=[pl.BlockSpec((1, 1, tb * PEER_SEL), lambda i: (i, 0, 0),
                               memory_space=pltpu.SMEM),
                  pl.BlockSpec((1, 1, tb * PEER_SEL), lambda i: (jnp.minimum(i + 1, nblk - 1), 0, 0),
                               memory_space=pltpu.SMEM),
                  pl.BlockSpec((tb, PEER_SEL), lambda i: (i, 0)),
                  pl.BlockSpec((tb, D), lambda i: (i, 0)),
                  vec, vec, vec,
                  pl.BlockSpec(memory_space=pl.ANY)],
        out_specs=pl.BlockSpec((tb, D), lambda i: (i, 0)),
        out_shape=jax.ShapeDtypeStruct((M, D), F32),
        scratch_shapes=[pltpu.VMEM((2, tb * PEER_SEL, 2 * D), F32),
                        pltpu.SemaphoreType.DMA((2,))],
        compiler_params=_params(("arbitrary",)),
        name="peer_experts",
    )(idx3, idx3, gate, x, shift.reshape(B, 1, D), scale.reshape(B, 1, D), gate_f.reshape(B, 1, D),
      table)


def _final_kernel(x_ref, g_ref, o_ref):
    o_ref[...] = _rms_rows(x_ref[...]) * g_ref[...]


def _final_norm(x, g, tm=512):
    M, D = x.shape
    tm = _tile(M, tm)
    return pl.pallas_call(
        _final_kernel,
        grid=(M // tm,),
        in_specs=[pl.BlockSpec((tm, D), lambda i: (i, 0)), pl.BlockSpec((1, D), lambda i: (0, 0))],
        out_specs=pl.BlockSpec((tm, D), lambda i: (i, 0)),
        out_shape=jax.ShapeDtypeStruct((M, D), F32),
        compiler_params=_params(("arbitrary",)),
        name="final_norm",
    )(x, g.reshape(1, D))


def kernel(x, c, mod_w, mod_b, a_in_w, a_conv_w, a_log, a_dt_bias, a_norm_g, a_out_w, kv_mod_w,
           kv_mod_b, kv_w, b_q_w, b_out_w, peer_q_w, peer_subkeys, peer_u, peer_v, final_g):
    B, S, D = x.shape
    depth = mod_w.shape[0]
    n_a = a_in_w.shape[0]
    M = B * S
    AW = A_HEADS * HEAD_DIM
    xf = x.reshape(M, D)

    mod = _mod_matmul(c, mod_w, mod_b)
    kv_mod = _mod_matmul(c, kv_mod_w[None], kv_mod_b[None])[0]
    k_v = None
    for layer in range(depth):
        sh_m, sc_m, g_m, sh_f, sc_f, g_f = (mod[layer, :, k * D:(k + 1) * D] for k in range(6))
        if layer < n_a:
            w_in = a_in_w[layer]
            proj = _matmul(xf, w_in, n_out=4 * AW, shift=sh_m, scale=sc_m, seq=S, name="gdn_in_proj")
            w_gate = jnp.pad(w_in[:, 4 * AW:], ((0, 0), (0, LANES - 2 * A_HEADS)))
            gates_raw = _matmul(xf, w_gate, shift=sh_m, scale=sc_m, seq=S, name="gdn_gate_proj")
            o = _gated_deltanet_core(proj.reshape(B, S, 4 * AW), gates_raw.reshape(B, S, LANES),
                                     a_conv_w[layer], a_log[layer], a_dt_bias[layer],
                                     a_norm_g[layer], B, S)
            xf = _matmul(o.reshape(M, AW), a_out_w[layer], res=xf, gate=g_m, seq=S,
                         name="gdn_out_proj")
        else:
            if k_v is None:
                k_v = _matmul(xf, kv_w, shift=kv_mod[:, :D], scale=kv_mod[:, D:], seq=S,
                              name="shared_kv_proj")
            j = layer - n_a
            q = _matmul(xf, b_q_w[j], shift=sh_m, scale=sc_m, seq=S, name="attn_q_proj")
            outs, lses = zip(*[_dilated_group(q, k_v, gi, B, S) for gi in range(len(B_GROUPS))])
            o = _combine_groups(outs, lses)
            xf = _matmul(o, b_out_w[j], res=xf, gate=g_m, seq=S, name="attn_out_proj")
        pq = _matmul(xf, peer_q_w[layer], shift=sh_f, scale=sc_f, seq=S, name="peer_q_proj")
        idx, gate = _peer_route(pq, peer_subkeys[layer])
        table = jnp.concatenate([peer_u[layer], peer_v[layer]], axis=1)
        xf = _peer_experts(xf, idx, gate, sh_f, sc_f, g_f, table, S)
    return _final_norm(xf, final_g).reshape(B, S, D)
```

```python
import functools

import jax
import jax.numpy as jnp
import numpy as np
from jax import lax
from jax.experimental import pallas as pl
from jax.experimental.pallas import tpu as pltpu

F32 = jnp.float32
BF16 = jnp.bfloat16

NORM_EPS = 1e-6
LANES = 128
SUBLANES = 8

A_HEADS = 16
HEAD_DIM = 128
A_CHUNK = 64
A_CONV = 4
GDN_HEADS = 2
PREP_UNROLL = 4
B_GROUPS = ((128, 1), (512, 4), (2048, 16))
B_HEADS = 16
B_KV_HEADS = 4
ATTN_BLOCK = 128
PEER_HEADS = 8
PEER_KEYS = 128
PEER_TOPK = 16
PEER_SEL = PEER_HEADS * PEER_TOPK

VMEM_LIMIT = 56 * 1024 * 1024


def _params(sem):
    return pltpu.CompilerParams(dimension_semantics=sem, vmem_limit_bytes=VMEM_LIMIT)


def _tile(n, pref):
    t = pref
    while t > LANES and n % t:
        t //= 2
    assert n % t == 0, (n, pref)
    return t


def _rms_rows(x):
    return x * lax.rsqrt(jnp.mean(x * x, axis=-1, keepdims=True) + NORM_EPS)


def _dot(a, b):
    return jnp.dot(a, b, preferred_element_type=F32)


def _dot_t(a, b):
    return lax.dot_general(a, b, (((1,), (1,)), ((), ())), preferred_element_type=F32)


def _split2(a):
    hi = a.astype(BF16)
    return hi, (a - hi.astype(F32)).astype(BF16)


def _split3(a):
    hi = a.astype(BF16)
    r = a - hi.astype(F32)
    mid = r.astype(BF16)
    return hi, mid, (r - mid.astype(F32)).astype(BF16)


def _dot3(a2, b2):
    (ah, al), (bh, bl) = a2, b2
    return _dot(ah, bh) + (_dot(ah, bl) + _dot(al, bh))


def _mod_kernel(c_ref, w_ref, b_ref, o_ref):
    c = c_ref[...]
    cs = (c * jax.nn.sigmoid(c)).astype(BF16)
    o_ref[0] = jnp.dot(cs, w_ref[0].astype(BF16), preferred_element_type=F32) + b_ref[0]


def _mod_matmul(c, w, b, tn=1024):
    L, D, N = w.shape
    B = c.shape[0]
    tn = _tile(N, tn)
    return pl.pallas_call(
        _mod_kernel,
        grid=(L, N // tn),
        in_specs=[pl.BlockSpec((B, D), lambda l, j: (0, 0)),
                  pl.BlockSpec((1, D, tn), lambda l, j: (l, 0, j)),
                  pl.BlockSpec((1, 1, tn), lambda l, j: (l, 0, j))],
        out_specs=pl.BlockSpec((1, B, tn), lambda l, j: (l, 0, j)),
        out_shape=jax.ShapeDtypeStruct((L, B, N), F32),
        compiler_params=_params(("arbitrary", "arbitrary")),
        name="adaln_mod",
    )(c, w, b.reshape(L, 1, N))


def _mm_kernel(*refs, modulate, residual):
    it = iter(refs)
    a_ref = next(it)
    if modulate:
        sh_ref, sc_ref = next(it), next(it)
    w_ref = next(it)
    if residual:
        res_ref, g_ref = next(it), next(it)
    o_ref = next(it)
    a_bf = next(it)

    @pl.when(pl.program_id(1) == 0)
    def _():
        a = a_ref[...]
        if modulate:
            a = _rms_rows(a) * (1.0 + sc_ref[0]) + sh_ref[0]
        a_bf[...] = a.astype(BF16)

    acc = jnp.dot(a_bf[...], w_ref[...].astype(BF16), preferred_element_type=F32)
    if residual:
        acc = res_ref[...] + g_ref[0] * acc
    o_ref[...] = acc


def _matmul(a, w, *, n_out=None, shift=None, scale=None, res=None, gate=None, seq=None,
            tm=1024, tn=512, name="matmul"):
    M, K = a.shape
    N = w.shape[1] if n_out is None else n_out
    tn = _tile(N, tn)
    tm = _tile(seq, tm)
    modulate = shift is not None
    residual = res is not None
    bpb = seq // tm
    args = [a]
    specs = [pl.BlockSpec((tm, K), lambda i, j: (i, 0))]
    if modulate:
        B = shift.shape[0]
        args += [shift.reshape(B, 1, K), scale.reshape(B, 1, K)]
        specs += [pl.BlockSpec((1, 1, K), lambda i, j: (i // bpb, 0, 0))] * 2
    args.append(w)
    specs.append(pl.BlockSpec((K, tn), lambda i, j: (0, j)))
    if residual:
        B = gate.shape[0]
        args += [res, gate.reshape(B, 1, N)]
        specs += [pl.BlockSpec((tm, tn), lambda i, j: (i, j)),
                  pl.BlockSpec((1, 1, tn), lambda i, j: (i // bpb, 0, j))]
    return pl.pallas_call(
        functools.partial(_mm_kernel, modulate=modulate, residual=residual),
        grid=(M // tm, N // tn),
        in_specs=specs,
        out_specs=pl.BlockSpec((tm, tn), lambda i, j: (i, j)),
        out_shape=jax.ShapeDtypeStruct((M, N), F32),
        scratch_shapes=[pltpu.VMEM((tm, K), BF16)],
        compiler_params=_params(("arbitrary", "arbitrary")),
        name=name,
    )(*args)


def _gdn_kernel(q_ref, k_ref, v_ref, z_ref, cwq_ref, cwk_ref, cwv_ref, gr_ref, alog_ref, dtb_ref,
                ng_ref, o_ref, q_s, k_s, g_s, beta_s, u_s, w_s, qk_s, cd_s):
    hp = pl.program_id(1)
    S = q_ref.shape[1]
    C = A_CHUNK
    n_chunks = S // C
    row = lax.broadcasted_iota(jnp.int32, (S, HEAD_DIM), 0)
    lane = lax.broadcasted_iota(jnp.int32, (1, LANES), 1)

    def conv_silu(x, w):
        y = x * w[A_CONV - 1:A_CONV, :]
        for back in range(1, A_CONV):
            xs = jnp.where(row >= back, pltpu.roll(x, back, axis=0), 0.0)
            y = y + xs * w[A_CONV - 1 - back:A_CONV - back, :]
        return y * jax.nn.sigmoid(y)

    def l2n(t):
        return t * lax.rsqrt(jnp.sum(t * t, axis=-1, keepdims=True) + NORM_EPS)

    gr = gr_ref[0]
    for j in range(GDN_HEADS):
        hs = slice(j * HEAD_DIM, (j + 1) * HEAD_DIM)
        h = hp * GDN_HEADS + j
        q_s[j] = l2n(conv_silu(q_ref[0, :, hs], cwq_ref[:, hs])) * (HEAD_DIM ** -0.5)
        k_s[j] = l2n(conv_silu(k_ref[0, :, hs], cwk_ref[:, hs]))
        vv = conv_silu(v_ref[0, :, hs], cwv_ref[:, hs])
        oh_b = (lane == h).astype(F32)
        oh_a = (lane == h + A_HEADS).astype(F32)
        b_raw = jnp.sum(gr * oh_b, axis=-1, keepdims=True)
        a_raw = jnp.sum(gr * oh_a, axis=-1, keepdims=True)
        a_log = jnp.sum(alog_ref[...] * oh_b, axis=-1, keepdims=True)
        dtb = jnp.sum(dtb_ref[...] * oh_b, axis=-1, keepdims=True)
        sp_in = a_raw + dtb
        softplus = jnp.maximum(sp_in, 0.0) + jnp.log1p(jnp.exp(-jnp.abs(sp_in)))
        g_s[j] = -jnp.exp(a_log) * softplus
        beta = jax.nn.sigmoid(b_raw)
        beta_s[j] = beta
        u_s[j] = vv * beta

    ri = lax.broadcasted_iota(jnp.int32, (C, C), 0)
    ci = lax.broadcasted_iota(jnp.int32, (C, C), 1)
    incl = ri >= ci
    strict = ri > ci
    lower = incl.astype(BF16)
    upper = (ri <= ci).astype(F32)
    ones = jnp.ones((C, C), BF16)

    def chunk_rows(c):
        return pl.ds(pl.multiple_of(c * C, C), C)

    def cd_rows(c, n):
        return pl.ds(pl.multiple_of(c * SUBLANES, SUBLANES), n)

    def prep_load(j, c):
        rows = chunk_rows(c)
        return g_s[j, rows, :], beta_s[j, rows, :], k_s[j, rows, :], q_s[j, rows, :], u_s[j, rows, :]

    def prep_compute(loaded):
        st = []
        for g_c, b_c, k_c, q_c, v_c in loaded:
            gb_hi, gb_lo = _split2(jnp.broadcast_to(g_c, (C, C)))
            gu_hi, gu_lo = _split2(g_c * upper)
            gcol = _dot(lower, gb_hi) + _dot(lower, gb_lo)
            grow = _dot(ones, gu_hi) + _dot(ones, gu_lo)
            k_bf = k_c.astype(BF16)
            st.append(dict(gcol=gcol, grow=grow, k_bf=k_bf, kk=_dot_t(k_bf, k_bf),
                           qk=_dot_t(q_c.astype(BF16), k_bf)))
        for s, (g_c, b_c, k_c, q_c, v_c) in zip(st, loaded):
            decay = jnp.exp(jnp.where(incl, s["gcol"] - s["grow"], -jnp.inf))
            gam = s["gcol"][:, 0:1]
            gam_last = s["grow"][:, C - 1:C]
            egam = jnp.exp(gam)
            a_mat = jnp.where(strict, s["kk"] * decay * b_c, 0.0)
            rhs = jnp.concatenate([v_c, k_c * (b_c * egam)], axis=1)
            s.update(qk=s["qk"] * decay, qd=q_c * egam, kd=k_c * jnp.exp(gam_last - gam),
                     cd=jnp.broadcast_to(jnp.exp(gam_last[0:SUBLANES, :]), (SUBLANES, HEAD_DIM)),
                     pw=_split2(a_mat), rhs=rhs)
        for s in st:
            s["x"] = s["rhs"] - _dot3(s["pw"], _split2(s["rhs"]))
        for _ in range(5):
            for s in st:
                s["pw"] = _split2(_dot3(s["pw"], s["pw"]))
            for s in st:
                s["x"] = s["x"] + _dot3(s["pw"], _split2(s["x"]))
        return [(s["x"][:, :HEAD_DIM], s["x"][:, HEAD_DIM:], s["qk"], s["qd"], s["kd"], s["cd"])
                for s in st]

    def prep_store(j, c, u, w, qk, qd, kd, cd):
        rows = chunk_rows(c)
        u_s[j, rows, :] = u
        w_s[j, rows, :] = w
        qk_s[j, rows, :] = qk
        q_s[j, rows, :] = qd
        k_s[j, rows, :] = kd
        cd_s[j, cd_rows(c, SUBLANES), :] = cd

    def prep(i, carry):
        work = [(j, i * PREP_UNROLL + k) for j in range(GDN_HEADS) for k in range(PREP_UNROLL)]
        loaded = [prep_load(j, c) for j, c in work]
        outs = prep_compute(loaded)
        for (j, c), out in zip(work, outs):
            prep_store(j, c, *out)
        return carry

    lax.fori_loop(0, n_chunks // PREP_UNROLL, prep, 0)

    ng = ng_ref[...]

    def scan(c, states):
        rows = chunk_rows(c)
        loaded = [(u_s[j, rows, :], w_s[j, rows, :], q_s[j, rows, :], qk_s[j, rows, :],
                   k_s[j, rows, :], cd_s[j, cd_rows(c, 1), :], z_ref[0, rows, j * HEAD_DIM:(j + 1) * HEAD_DIM])
                  for j in range(GDN_HEADS)]
        new_states, outs = [], []
        for state, (u_c, w_c, qd_c, qk_c, kd_c, cd, z_c) in zip(states, loaded):
            st_bf = state.astype(BF16)
            v_new = u_c - _dot(w_c.astype(BF16), st_bf)
            vn_bf = v_new.astype(BF16)
            o_c = _dot(qd_c.astype(BF16), st_bf) + _dot(qk_c.astype(BF16), vn_bf)
            new_states.append(state * cd + lax.dot_general(
                kd_c.astype(BF16), vn_bf, (((0,), (0,)), ((), ())), preferred_element_type=F32))
            outs.append(_rms_rows(o_c) * ng * (z_c * jax.nn.sigmoid(z_c)))
        for j, out in enumerate(outs):
            o_ref[0, rows, j * HEAD_DIM:(j + 1) * HEAD_DIM] = out
        return tuple(new_states)

    lax.fori_loop(0, n_chunks, scan,
                  tuple(jnp.zeros((HEAD_DIM, HEAD_DIM), F32) for _ in range(GDN_HEADS)))


def _gated_deltanet_core(proj, gates_raw, conv_w, a_log, dt_bias, norm_g, B, S):
    H = A_HEADS
    C = A_CHUNK
    NH = GDN_HEADS
    HB = H // NH
    assert S % (C * PREP_UNROLL) == 0 and H % NH == 0

    def col(off):
        return pl.BlockSpec((1, S, NH * HEAD_DIM), lambda b, h: (b, 0, off + h))

    def cw(off):
        return pl.BlockSpec((A_CONV, NH * HEAD_DIM), lambda b, h: (0, off + h))

    row128 = pl.BlockSpec((1, LANES), lambda b, h: (0, 0))
    pad = lambda t: jnp.pad(t.reshape(1, -1), ((0, 0), (0, LANES - t.shape[-1])))
    head_buf = pltpu.VMEM((NH, S, HEAD_DIM), F32)
    return pl.pallas_call(
        _gdn_kernel,
        grid=(B, HB),
        in_specs=[col(0), col(HB), col(2 * HB), col(3 * HB), cw(0), cw(HB), cw(2 * HB),
                  pl.BlockSpec((1, S, LANES), lambda b, h: (b, 0, 0)),
                  row128, row128, row128],
        out_specs=pl.BlockSpec((1, S, NH * HEAD_DIM), lambda b, h: (b, 0, h)),
        out_shape=jax.ShapeDtypeStruct((B, S, H * HEAD_DIM), F32),
        scratch_shapes=[head_buf, head_buf,
                        pltpu.VMEM((NH, S, 1), F32), pltpu.VMEM((NH, S, 1), F32),
                        head_buf, head_buf,
                        pltpu.VMEM((NH, S, C), F32),
                        pltpu.VMEM((NH, S // C * SUBLANES, HEAD_DIM), F32)],
        compiler_params=_params(("arbitrary", "arbitrary")),
        name="gated_deltanet",
    )(proj, proj, proj, proj, conv_w, conv_w, conv_w, gates_raw, pad(a_log), pad(dt_bias),
      norm_g.reshape(1, HEAD_DIM))


def _attn_kernel(q_ref, kp_ref, kc_ref, vp_ref, vc_ref, o_ref, lse_ref, *, dil, slopes):
    n = pl.program_id(2)
    T = ATTN_BLOCK
    qi = lax.broadcasted_iota(jnp.int32, (T, 2 * T), 0) + T
    kj = lax.broadcasted_iota(jnp.int32, (T, 2 * T), 1)
    dist = qi - kj
    mask = (dist >= 0) & (dist <= T) & ((kj >= T) | (n > 0))
    distf = (dil * dist).astype(F32)
    lane = lax.broadcasted_iota(jnp.int32, (T, LANES), 1)
    lse_all = jnp.zeros((T, LANES), F32)
    grp = B_HEADS // B_KV_HEADS
    for kvh in range(B_KV_HEADS):
        cs = slice(kvh * HEAD_DIM, (kvh + 1) * HEAD_DIM)
        kcat = jnp.concatenate([kp_ref[0, :, cs], kc_ref[0, :, cs]], axis=0).astype(BF16)
        vcat = jnp.concatenate([vp_ref[0, :, cs], vc_ref[0, :, cs]], axis=0).astype(BF16)
        for gq in range(grp):
            hh = kvh * grp + gq
            hs = slice(hh * HEAD_DIM, (hh + 1) * HEAD_DIM)
            s = _dot_t(q_ref[0, :, hs].astype(BF16), kcat) * (HEAD_DIM ** -0.5)
            s = jnp.where(mask, s - slopes[hh] * distf, -jnp.inf)
            m = jnp.max(s, axis=-1, keepdims=True)
            e = jnp.exp(s - m)
            l = jnp.sum(e, axis=-1, keepdims=True)
            p = e / l
            o_ref[0, :, hs] = jnp.dot(p.astype(BF16), vcat, preferred_element_type=F32)
            lse_all = jnp.where(lane == hh, m + jnp.log(l), lse_all)
    lse_ref[0] = lse_all


def _dilated_group(q, kv, gi, B, S):
    G = len(B_GROUPS)
    window, dil = B_GROUPS[gi]
    assert window // dil == ATTN_BLOCK and S % (dil * ATTN_BLOCK) == 0
    L = S // dil
    nb = L // ATTN_BLOCK
    W = B_HEADS * HEAD_DIM
    KVW = B_KV_HEADS * HEAD_DIM
    n_slopes = G * B_HEADS
    slopes = tuple(2.0 ** (-8.0 * (gi * B_HEADS + hh + 1) / n_slopes) for hh in range(B_HEADS))
    qv = q.reshape(B, L, dil * G * W)
    kvv = kv.reshape(B, L, dil * 2 * G * KVW)
    T = ATTN_BLOCK

    def kspec(off, prev):
        if prev:
            return pl.BlockSpec((1, T, KVW), lambda b, r, n: (b, jnp.maximum(n - 1, 0), r * 2 * G + off))
        return pl.BlockSpec((1, T, KVW), lambda b, r, n: (b, n, r * 2 * G + off))

    o, lse = pl.pallas_call(
        functools.partial(_attn_kernel, dil=dil, slopes=slopes),
        grid=(B, dil, nb),
        in_specs=[pl.BlockSpec((1, T, W), lambda b, r, n: (b, n, r * G + gi)),
                  kspec(gi, True), kspec(gi, False), kspec(G + gi, True), kspec(G + gi, False)],
        out_specs=[pl.BlockSpec((1, T, W), lambda b, r, n: (b, n, r)),
                   pl.BlockSpec((1, T, LANES), lambda b, r, n: (b, n, r))],
        out_shape=[jax.ShapeDtypeStruct((B, L, dil * W), F32),
                   jax.ShapeDtypeStruct((B, L, dil * LANES), F32)],
        compiler_params=_params(("arbitrary", "arbitrary", "arbitrary")),
        name=f"dilated_attn_g{gi}",
    )(qv, kvv, kvv, kvv, kvv)
    return o.reshape(B * S, W), lse.reshape(B * S, LANES)


def _combine_kernel(o0, o1, o2, l0, l1, l2, out_ref):
    ls = [l0[...], l1[...], l2[...]]
    m = jnp.maximum(jnp.maximum(ls[0], ls[1]), ls[2])
    es = [jnp.exp(l - m) for l in ls]
    inv = 1.0 / (es[0] + es[1] + es[2])
    ws = [e * inv for e in es]
    for hh in range(B_HEADS):
        hs = slice(hh * HEAD_DIM, (hh + 1) * HEAD_DIM)
        out_ref[:, hs] = (ws[0][:, hh:hh + 1] * o0[:, hs] + ws[1][:, hh:hh + 1] * o1[:, hs]
                          + ws[2][:, hh:hh + 1] * o2[:, hs])


def _combine_groups(outs, lses, tm=512):
    M, W = outs[0].shape
    tm = _tile(M, tm)
    ospec = pl.BlockSpec((tm, W), lambda i: (i, 0))
    lspec = pl.BlockSpec((tm, LANES), lambda i: (i, 0))
    return pl.pallas_call(
        _combine_kernel,
        grid=(M // tm,),
        in_specs=[ospec] * 3 + [lspec] * 3,
        out_specs=ospec,
        out_shape=jax.ShapeDtypeStruct((M, W), F32),
        compiler_params=_params(("arbitrary",)),
        name="attn_combine",
    )(*outs, *lses)


def _topk_rows(s, pos, k):
    big = float(s.shape[0])
    vals, ids = [], []
    for _ in range(k):
        m = jnp.max(s, axis=0, keepdims=True)
        i = jnp.min(jnp.where(s == m, pos, big), axis=0, keepdims=True)
        vals.append(m)
        ids.append(i)
        s = jnp.where(pos == i, -jnp.inf, s)
    return jnp.concatenate(vals, axis=0), jnp.concatenate(ids, axis=0)


_CAND_PAIRS = [(a, b) for a in range(PEER_TOPK) for b in range(PEER_TOPK)
               if (a + 1) * (b + 1) <= PEER_TOPK]
_N_CAND = len(_CAND_PAIRS)
_CAND_ROWS = -(-_N_CAND // SUBLANES) * SUBLANES


def _cand_select():
    sel = np.zeros((2, _CAND_ROWS, PEER_TOPK), np.float32)
    for r, (a, b) in enumerate(_CAND_PAIRS):
        sel[0, r, a] = 1.0
        sel[1, r, b] = 1.0
    return sel


def _pick_rows(sel, rows):
    hi, mid, lo = _split3(rows)
    return (_dot(sel, hi) + _dot(sel, mid)) + _dot(sel, lo)


def _peer_route_kernel(q_ref, sk_ref, sel_ref, idx_ref, gate_ref):
    T = q_ref.shape[0]
    K = PEER_TOPK
    kpos = lax.broadcasted_iota(jnp.int32, (PEER_KEYS, T), 0).astype(F32)
    cpos_i = lax.broadcasted_iota(jnp.int32, (_CAND_ROWS, T), 0)
    cpos = cpos_i.astype(F32)
    real = cpos_i < _N_CAND
    sk = [sk_ref[0].astype(BF16), sk_ref[1].astype(BF16)]
    sel_a, sel_b = sel_ref[0].astype(BF16), sel_ref[1].astype(BF16)
    idx_rows, gate_rows = [], []
    for hd in range(PEER_HEADS):
        sv, si = [], []
        for p in range(2):
            c0 = (hd * 2 + p) * HEAD_DIM
            sub = _dot_t(sk[p], q_ref[:, c0:c0 + HEAD_DIM].astype(BF16))
            v, i = _topk_rows(sub, kpos, K)
            sv.append(v)
            si.append(i)
        cand_s = jnp.where(real, _pick_rows(sel_a, sv[0]) + _pick_rows(sel_b, sv[1]), -jnp.inf)
        cand_i = _pick_rows(sel_a, si[0]) * float(PEER_KEYS) + _pick_rows(sel_b, si[1])
        top_s, pos = _topk_rows(cand_s, cpos, K)
        e_rows = [jnp.sum(jnp.where(cpos == pos[r:r + 1, :], cand_i, 0.0), axis=0, keepdims=True)
                  for r in range(K)]
        ex = jnp.exp(top_s - top_s[0:1, :])
        gate_rows.append(ex / jnp.sum(ex, axis=0, keepdims=True))
        idx_rows.append(jnp.concatenate(e_rows, axis=0))
    idx_ref[...] = jnp.concatenate(idx_rows, axis=0).T.astype(jnp.int32)
    gate_ref[...] = jnp.concatenate(gate_rows, axis=0).T


def _peer_route(q, subkeys, tt=128):
    M, W = q.shape
    return pl.pallas_call(
        _peer_route_kernel,
        grid=(M // tt,),
        in_specs=[pl.BlockSpec((tt, W), lambda i: (i, 0)),
                  pl.BlockSpec(subkeys.shape, lambda i: (0, 0, 0)),
                  pl.BlockSpec((2, _CAND_ROWS, PEER_TOPK), lambda i: (0, 0, 0))],
        out_specs=[pl.BlockSpec((tt, PEER_SEL), lambda i: (i, 0))] * 2,
        out_shape=[jax.ShapeDtypeStruct((M, PEER_SEL), jnp.int32),
                   jax.ShapeDtypeStruct((M, PEER_SEL), F32)],
        compiler_params=_params(("arbitrary",)),
        name="peer_route",
    )(q, subkeys, jnp.asarray(_cand_select()))


_FOLD_ORDER = (6, 2, 4, 0, 7, 3, 5, 1)
_FOLD_SUBLANE = tuple(_FOLD_ORDER.index(m) for m in range(SUBLANES))


def _fold8(p, sub):
    q = [t + pltpu.roll(t, 4, axis=0) for t in p]
    m = [jnp.where(sub < 4, q[2 * j], q[2 * j + 1]) for j in range(4)]
    r = [t + pltpu.roll(t, 2, axis=0) for t in m]
    n = [jnp.where((sub & 2) != 0, r[2 * k], pltpu.roll(r[2 * k + 1], 6, axis=0)) for k in range(2)]
    t = [v + pltpu.roll(v, 1, axis=0) for v in n]
    return jnp.where((sub & 1) != 0, t[0], pltpu.roll(t[1], 7, axis=0))


def _peer_expert_kernel(idxc_ref, idxn_ref, gate_ref, x_ref, sh_ref, sc_ref, gf_ref, tab_ref,
                        o_ref, buf, sem, coef0, coef1, *, tb):
    i = pl.program_id(0)
    n = pl.num_programs(0)
    slot = lax.rem(i, 2)
    nslot = 1 - slot
    groups = PEER_SEL // SUBLANES
    half = x_ref.shape[1] // 2
    d_model = x_ref.shape[1] * LANES

    def slab_copy(e, s, r, t):
        return pltpu.make_async_copy(tab_ref.at[e], buf.at[s, r], sem.at[s * tb + t])

    def token_wait(s, t):
        rows = pl.ds(t * PEER_SEL, PEER_SEL)
        pltpu.make_async_copy(buf.at[s, rows], buf.at[s, rows], sem.at[s * tb + t]).wait()

    @pl.when(i == 0)
    def _():
        def body(r8, carry):
            for k in range(SUBLANES):
                slab_copy(idxc_ref[0, 0, r8 * SUBLANES + k], 0, r8 * SUBLANES + _FOLD_ORDER[k],
                          r8 // groups).start()
            return carry
        lax.fori_loop(0, tb * groups, body, 0)

    x = x_ref[...]
    ms = jnp.sum(jnp.sum(x * x, axis=2, keepdims=True), axis=1, keepdims=True) * (1.0 / d_model)
    hmod = x * lax.rsqrt(ms + NORM_EPS) * (1.0 + sc_ref[...]) + sh_ref[...]
    gf = gf_ref[0]
    ri = lax.broadcasted_iota(jnp.int32, (PEER_SEL, PEER_SEL), 0)
    ci = lax.broadcasted_iota(jnp.int32, (PEER_SEL, PEER_SEL), 1)
    diag = ri == ci
    sub = lax.broadcasted_iota(jnp.int32, (SUBLANES, LANES), 0)
    hi_half = jnp.uint32(0xFFFF0000)
    coef_refs = (coef0, coef1)

    acc = None
    for t in range(tb + 1):
        if t < tb:
            token_wait(slot, t)
            h0, h1 = hmod[t, :half, :], hmod[t, half:, :]
            folded = []
        if t >= 1:
            acc = [jnp.zeros((half, LANES), F32), jnp.zeros((half, LANES), F32)]
        for g in range(groups):
            if t < tb:
                r0 = t * PEER_SEL + g * SUBLANES
                for k in range(SUBLANES):
                    slab_copy(idxn_ref[0, 0, r0 + k], nslot, r0 + _FOLD_ORDER[k], t).start(
                        priority=k % 2)
                parts = []
                for m in range(SUBLANES):
                    u = lax.bitcast_convert_type(buf[slot, r0 + m] << 16, F32)
                    parts.append(u[:half] * h0 + u[half:] * h1)
                folded.append(_fold8(parts, sub))
            if t >= 1:
                r0 = (t - 1) * PEER_SEL + g * SUBLANES
                cref = coef_refs[(t - 1) % 2]
                prods = [[], []]
                for m in range(SUBLANES):
                    row = g * SUBLANES + _FOLD_SUBLANE[m]
                    cm = jnp.broadcast_to(cref[row:row + 1, :], (half, LANES))
                    v = lax.bitcast_convert_type(buf[slot, r0 + m] & hi_half, F32)
                    prods[0].append(cm * v[:half])
                    prods[1].append(cm * v[half:])
                for hf in range(2):
                    ps = prods[hf]
                    acc[hf] = acc[hf] + (((ps[0] + ps[1]) + (ps[2] + ps[3]))
                                         + ((ps[4] + ps[5]) + (ps[6] + ps[7])))
        if t < tb:
            act = jnp.sum(jnp.concatenate(folded, axis=0), axis=-1, keepdims=True)
            gate_col = jnp.sum(jnp.where(diag, gate_ref[t:t + 1, :], 0.0), axis=-1, keepdims=True)
            coef_refs[t % 2][...] = jnp.broadcast_to(gate_col * jax.nn.gelu(act), (PEER_SEL, LANES))
        if t >= 1:
            o_ref[t - 1] = x[t - 1] + gf * jnp.concatenate(acc, axis=0)

    @pl.when(i == n - 1)
    def _():
        for t in range(tb):
            token_wait(nslot, t)


def _pack_expert_tables(u, v):
    lo = lax.bitcast_convert_type(u.astype(jnp.bfloat16), jnp.uint16).astype(jnp.uint32)
    hi = lax.bitcast_convert_type(v.astype(jnp.bfloat16), jnp.uint16).astype(jnp.uint32)
    return (lo | (hi << 16)).reshape(u.shape[0], u.shape[1] // LANES, LANES)


def _peer_experts(x, idx, gate, shift, scale, gate_f, table, seq, tb=8):
    M, D = x.shape
    B = shift.shape[0]
    R = D // LANES
    assert R % (2 * SUBLANES) == 0 and R == 2 * SUBLANES
    nblk = M // tb
    bpb = seq // tb
    idx3 = idx.reshape(nblk, 1, tb * PEER_SEL)
    vec = pl.BlockSpec((1, R, LANES), lambda i: (i // bpb, 0, 0))
    out = pl.pallas_call(
        functools.partial(_peer_expert_kernel, tb=tb),
        grid=(nblk,),
        in_specs=[pl.BlockSpec((1, 1, tb * PEER_SEL), lambda i: (i, 0, 0),
                               memory_space=pltpu.SMEM),
                  pl.BlockSpec((1, 1, tb * PEER_SEL), lambda i: (jnp.minimum(i + 1, nblk - 1), 0, 0),
                               memory_space=pltpu.SMEM),
                  pl.BlockSpec((tb, PEER_SEL), lambda i: (i, 0)),
                  pl.BlockSpec((tb, R, LANES), lambda i: (i, 0, 0)),
                  vec, vec, vec,
                  pl.BlockSpec(memory_space=pl.ANY)],
        out_specs=pl.BlockSpec((tb, R, LANES), lambda i: (i, 0, 0)),
        out_shape=jax.ShapeDtypeStruct((M, R, LANES), F32),
        scratch_shapes=[pltpu.VMEM((2, tb * PEER_SEL, R, LANES), jnp.uint32),
                        pltpu.SemaphoreType.DMA((2 * tb,)),
                        pltpu.VMEM((PEER_SEL, LANES), F32), pltpu.VMEM((PEER_SEL, LANES), F32)],
        compiler_params=_params(("arbitrary",)),
        name="peer_experts",
    )(idx3, idx3, gate, x.reshape(M, R, LANES), shift.reshape(B, R, LANES),
      scale.reshape(B, R, LANES), gate_f.reshape(B, R, LANES), table)
    return out.reshape(M, D)


def _final_kernel(x_ref, g_ref, o_ref):
    o_ref[...] = _rms_rows(x_ref[...]) * g_ref[...]


def _final_norm(x, g, tm=512):
    M, D = x.shape
    tm = _tile(M, tm)
    return pl.pallas_call(
        _final_kernel,
        grid=(M // tm,),
        in_specs=[pl.BlockSpec((tm, D), lambda i: (i, 0)), pl.BlockSpec((1, D), lambda i: (0, 0))],
        out_specs=pl.BlockSpec((tm, D), lambda i: (i, 0)),
        out_shape=jax.ShapeDtypeStruct((M, D), F32),
        compiler_params=_params(("arbitrary",)),
        name="final_norm",
    )(x, g.reshape(1, D))


def kernel(x, c, mod_w, mod_b, a_in_w, a_conv_w, a_log, a_dt_bias, a_norm_g, a_out_w, kv_mod_w,
           kv_mod_b, kv_w, b_q_w, b_out_w, peer_q_w, peer_subkeys, peer_u, peer_v, final_g):
    B, S, D = x.shape
    depth = mod_w.shape[0]
    n_a = a_in_w.shape[0]
    M = B * S
    AW = A_HEADS * HEAD_DIM
    xf = x.reshape(M, D)

    mod = _mod_matmul(c, mod_w, mod_b)
    kv_mod = _mod_matmul(c, kv_mod_w[None], kv_mod_b[None])[0]
    k_v = None
    for layer in range(depth):
        sh_m, sc_m, g_m, sh_f, sc_f, g_f = (mod[layer, :, k * D:(k + 1) * D] for k in range(6))
        if layer < n_a:
            w_in = a_in_w[layer]
            proj = _matmul(xf, w_in, n_out=4 * AW, shift=sh_m, scale=sc_m, seq=S, name="gdn_in_proj")
            w_gate = jnp.pad(w_in[:, 4 * AW:], ((0, 0), (0, LANES - 2 * A_HEADS)))
            gates_raw = _matmul(xf, w_gate, shift=sh_m, scale=sc_m, seq=S, name="gdn_gate_proj")
            o = _gated_deltanet_core(proj.reshape(B, S, 4 * AW), gates_raw.reshape(B, S, LANES),
                                     a_conv_w[layer], a_log[layer], a_dt_bias[layer],
                                     a_norm_g[layer], B, S)
            xf = _matmul(o.reshape(M, AW), a_out_w[layer], res=xf, gate=g_m, seq=S,
                         name="gdn_out_proj")
        else:
            if k_v is None:
                k_v = _matmul(xf, kv_w, shift=kv_mod[:, :D], scale=kv_mod[:, D:], seq=S,
                              name="shared_kv_proj")
            j = layer - n_a
            q = _matmul(xf, b_q_w[j], shift=sh_m, scale=sc_m, seq=S, name="attn_q_proj")
            outs, lses = zip(*[_dilated_group(q, k_v, gi, B, S) for gi in range(len(B_GROUPS))])
            o = _combine_groups(outs, lses)
            xf = _matmul(o, b_out_w[j], res=xf, gate=g_m, seq=S, name="attn_out_proj")
        pq = _matmul(xf, peer_q_w[layer], shift=sh_f, scale=sc_f, seq=S, name="peer_q_proj")
        idx, gate = _peer_route(pq, peer_subkeys[layer])
        table = _pack_expert_tables(peer_u[layer], peer_v[layer])
        xf = _peer_experts(xf, idx, gate, sh_f, sc_f, g_f, table, S)
    return _final_norm(xf, final_g).reshape(B, S, D)
```

```python
import functools

import jax
import jax.numpy as jnp
import numpy as np
from jax import lax
from jax.experimental import pallas as pl
from jax.experimental.pallas import tpu as pltpu

F32 = jnp.float32
BF16 = jnp.bfloat16

NORM_EPS = 1e-6
LANES = 128
SUBLANES = 8

A_HEADS = 16
HEAD_DIM = 128
A_CHUNK = 64
A_CONV = 4
GDN_HEADS = 2
PREP_UNROLL = 4
B_GROUPS = ((128, 1), (512, 4), (2048, 16))
B_HEADS = 16
B_KV_HEADS = 4
ATTN_BLOCK = 128
PEER_HEADS = 8
PEER_KEYS = 128
PEER_TOPK = 16
PEER_SEL = PEER_HEADS * PEER_TOPK
COEF_LAG = 1

VMEM_LIMIT = 56 * 1024 * 1024


def _params(sem):
    return pltpu.CompilerParams(dimension_semantics=sem, vmem_limit_bytes=VMEM_LIMIT)


def _tile(n, pref):
    t = pref
    while t > LANES and n % t:
        t //= 2
    assert n % t == 0, (n, pref)
    return t


def _rms_rows(x):
    return x * lax.rsqrt(jnp.mean(x * x, axis=-1, keepdims=True) + NORM_EPS)


def _dot(a, b):
    return jnp.dot(a, b, preferred_element_type=F32)


def _dot_t(a, b):
    return lax.dot_general(a, b, (((1,), (1,)), ((), ())), preferred_element_type=F32)


def _split2(a):
    hi = a.astype(BF16)
    return hi, (a - hi.astype(F32)).astype(BF16)


def _split3(a):
    hi = a.astype(BF16)
    r = a - hi.astype(F32)
    mid = r.astype(BF16)
    return hi, mid, (r - mid.astype(F32)).astype(BF16)


def _dot3(a2, b2):
    (ah, al), (bh, bl) = a2, b2
    return _dot(ah, bh) + (_dot(ah, bl) + _dot(al, bh))


def _mod_kernel(c_ref, w_ref, b_ref, o_ref):
    c = c_ref[...]
    cs = (c * jax.nn.sigmoid(c)).astype(BF16)
    o_ref[0] = jnp.dot(cs, w_ref[0].astype(BF16), preferred_element_type=F32) + b_ref[0]


def _mod_matmul(c, w, b, tn=1024):
    L, D, N = w.shape
    B = c.shape[0]
    tn = _tile(N, tn)
    return pl.pallas_call(
        _mod_kernel,
        grid=(L, N // tn),
        in_specs=[pl.BlockSpec((B, D), lambda l, j: (0, 0)),
                  pl.BlockSpec((1, D, tn), lambda l, j: (l, 0, j)),
                  pl.BlockSpec((1, 1, tn), lambda l, j: (l, 0, j))],
        out_specs=pl.BlockSpec((1, B, tn), lambda l, j: (l, 0, j)),
        out_shape=jax.ShapeDtypeStruct((L, B, N), F32),
        compiler_params=_params(("arbitrary", "arbitrary")),
        name="adaln_mod",
    )(c, w, b.reshape(L, 1, N))


def _mm_kernel(*refs, modulate, residual):
    it = iter(refs)
    a_ref = next(it)
    if modulate:
        sh_ref, sc_ref = next(it), next(it)
    w_ref = next(it)
    if residual:
        res_ref, g_ref = next(it), next(it)
    o_ref = next(it)
    a_bf = next(it)

    @pl.when(pl.program_id(1) == 0)
    def _():
        a = a_ref[...]
        if modulate:
            a = _rms_rows(a) * (1.0 + sc_ref[0]) + sh_ref[0]
        a_bf[...] = a.astype(BF16)

    acc = jnp.dot(a_bf[...], w_ref[...].astype(BF16), preferred_element_type=F32)
    if residual:
        acc = res_ref[...] + g_ref[0] * acc
    o_ref[...] = acc


def _matmul(a, w, *, n_out=None, shift=None, scale=None, res=None, gate=None, seq=None,
            tm=1024, tn=512, name="matmul"):
    M, K = a.shape
    N = w.shape[1] if n_out is None else n_out
    tn = _tile(N, tn)
    tm = _tile(seq, tm)
    modulate = shift is not None
    residual = res is not None
    bpb = seq // tm
    args = [a]
    specs = [pl.BlockSpec((tm, K), lambda i, j: (i, 0))]
    if modulate:
        B = shift.shape[0]
        args += [shift.reshape(B, 1, K), scale.reshape(B, 1, K)]
        specs += [pl.BlockSpec((1, 1, K), lambda i, j: (i // bpb, 0, 0))] * 2
    args.append(w)
    specs.append(pl.BlockSpec((K, tn), lambda i, j: (0, j)))
    if residual:
        B = gate.shape[0]
        args += [res, gate.reshape(B, 1, N)]
        specs += [pl.BlockSpec((tm, tn), lambda i, j: (i, j)),
                  pl.BlockSpec((1, 1, tn), lambda i, j: (i // bpb, 0, j))]
    return pl.pallas_call(
        functools.partial(_mm_kernel, modulate=modulate, residual=residual),
        grid=(M // tm, N // tn),
        in_specs=specs,
        out_specs=pl.BlockSpec((tm, tn), lambda i, j: (i, j)),
        out_shape=jax.ShapeDtypeStruct((M, N), F32),
        scratch_shapes=[pltpu.VMEM((tm, K), BF16)],
        compiler_params=_params(("arbitrary", "arbitrary")),
        name=name,
    )(*args)


def _gdn_kernel(q_ref, k_ref, v_ref, z_ref, cwq_ref, cwk_ref, cwv_ref, gr_ref, alog_ref, dtb_ref,
                ng_ref, o_ref, q_s, k_s, g_s, beta_s, u_s, w_s, qk_s, cd_s):
    hp = pl.program_id(1)
    S = q_ref.shape[1]
    C = A_CHUNK
    n_chunks = S // C
    row = lax.broadcasted_iota(jnp.int32, (S, HEAD_DIM), 0)
    lane = lax.broadcasted_iota(jnp.int32, (1, LANES), 1)

    def conv_silu(x, w):
        y = x * w[A_CONV - 1:A_CONV, :]
        for back in range(1, A_CONV):
            xs = jnp.where(row >= back, pltpu.roll(x, back, axis=0), 0.0)
            y = y + xs * w[A_CONV - 1 - back:A_CONV - back, :]
        return y * jax.nn.sigmoid(y)

    def l2n(t):
        return t * lax.rsqrt(jnp.sum(t * t, axis=-1, keepdims=True) + NORM_EPS)

    gr = gr_ref[0]
    for j in range(GDN_HEADS):
        hs = slice(j * HEAD_DIM, (j + 1) * HEAD_DIM)
        h = hp * GDN_HEADS + j
        q_s[j] = l2n(conv_silu(q_ref[0, :, hs], cwq_ref[:, hs])) * (HEAD_DIM ** -0.5)
        k_s[j] = l2n(conv_silu(k_ref[0, :, hs], cwk_ref[:, hs]))
        vv = conv_silu(v_ref[0, :, hs], cwv_ref[:, hs])
        oh_b = (lane == h).astype(F32)
        oh_a = (lane == h + A_HEADS).astype(F32)
        b_raw = jnp.sum(gr * oh_b, axis=-1, keepdims=True)
        a_raw = jnp.sum(gr * oh_a, axis=-1, keepdims=True)
        a_log = jnp.sum(alog_ref[...] * oh_b, axis=-1, keepdims=True)
        dtb = jnp.sum(dtb_ref[...] * oh_b, axis=-1, keepdims=True)
        sp_in = a_raw + dtb
        softplus = jnp.maximum(sp_in, 0.0) + jnp.log1p(jnp.exp(-jnp.abs(sp_in)))
        g_s[j] = -jnp.exp(a_log) * softplus
        beta = jax.nn.sigmoid(b_raw)
        beta_s[j] = beta
        u_s[j] = vv * beta

    ri = lax.broadcasted_iota(jnp.int32, (C, C), 0)
    ci = lax.broadcasted_iota(jnp.int32, (C, C), 1)
    incl = ri >= ci
    strict = ri > ci
    lower = incl.astype(BF16)
    upper = (ri <= ci).astype(F32)
    ones = jnp.ones((C, C), BF16)

    def chunk_rows(c):
        return pl.ds(pl.multiple_of(c * C, C), C)

    def cd_rows(c, n):
        return pl.ds(pl.multiple_of(c * SUBLANES, SUBLANES), n)

    def prep_load(j, c):
        rows = chunk_rows(c)
        return g_s[j, rows, :], beta_s[j, rows, :], k_s[j, rows, :], q_s[j, rows, :], u_s[j, rows, :]

    def prep_compute(loaded):
        st = []
        for g_c, b_c, k_c, q_c, v_c in loaded:
            gb_hi, gb_lo = _split2(jnp.broadcast_to(g_c, (C, C)))
            gu_hi, gu_lo = _split2(g_c * upper)
            gcol = _dot(lower, gb_hi) + _dot(lower, gb_lo)
            grow = _dot(ones, gu_hi) + _dot(ones, gu_lo)
            k_bf = k_c.astype(BF16)
            st.append(dict(gcol=gcol, grow=grow, k_bf=k_bf, kk=_dot_t(k_bf, k_bf),
                           qk=_dot_t(q_c.astype(BF16), k_bf)))
        for s, (g_c, b_c, k_c, q_c, v_c) in zip(st, loaded):
            decay = jnp.exp(jnp.where(incl, s["gcol"] - s["grow"], -jnp.inf))
            gam = s["gcol"][:, 0:1]
            gam_last = s["grow"][:, C - 1:C]
            egam = jnp.exp(gam)
            a_mat = jnp.where(strict, s["kk"] * decay * b_c, 0.0)
            rhs = jnp.concatenate([v_c, k_c * (b_c * egam)], axis=1)
            s.update(qk=s["qk"] * decay, qd=q_c * egam, kd=k_c * jnp.exp(gam_last - gam),
                     cd=jnp.broadcast_to(jnp.exp(gam_last[0:SUBLANES, :]), (SUBLANES, HEAD_DIM)),
                     pw=_split2(a_mat), rhs=rhs)
        for s in st:
            s["x"] = s["rhs"] - _dot3(s["pw"], _split2(s["rhs"]))
        for _ in range(5):
            for s in st:
                s["pw"] = _split2(_dot3(s["pw"], s["pw"]))
            for s in st:
                s["x"] = s["x"] + _dot3(s["pw"], _split2(s["x"]))
        return [(s["x"][:, :HEAD_DIM], s["x"][:, HEAD_DIM:], s["qk"], s["qd"], s["kd"], s["cd"])
                for s in st]

    def prep_store(j, c, u, w, qk, qd, kd, cd):
        rows = chunk_rows(c)
        u_s[j, rows, :] = u
        w_s[j, rows, :] = w
        qk_s[j, rows, :] = qk
        q_s[j, rows, :] = qd
        k_s[j, rows, :] = kd
        cd_s[j, cd_rows(c, SUBLANES), :] = cd

    def prep(i, carry):
        work = [(j, i * PREP_UNROLL + k) for j in range(GDN_HEADS) for k in range(PREP_UNROLL)]
        loaded = [prep_load(j, c) for j, c in work]
        outs = prep_compute(loaded)
        for (j, c), out in zip(work, outs):
            prep_store(j, c, *out)
        return carry

    lax.fori_loop(0, n_chunks // PREP_UNROLL, prep, 0)

    ng = ng_ref[...]

    def scan(c, states):
        rows = chunk_rows(c)
        loaded = [(u_s[j, rows, :], w_s[j, rows, :], q_s[j, rows, :], qk_s[j, rows, :],
                   k_s[j, rows, :], cd_s[j, cd_rows(c, 1), :], z_ref[0, rows, j * HEAD_DIM:(j + 1) * HEAD_DIM])
                  for j in range(GDN_HEADS)]
        new_states, outs = [], []
        for state, (u_c, w_c, qd_c, qk_c, kd_c, cd, z_c) in zip(states, loaded):
            st_bf = state.astype(BF16)
            v_new = u_c - _dot(w_c.astype(BF16), st_bf)
            vn_bf = v_new.astype(BF16)
            o_c = _dot(qd_c.astype(BF16), st_bf) + _dot(qk_c.astype(BF16), vn_bf)
            new_states.append(state * cd + lax.dot_general(
                kd_c.astype(BF16), vn_bf, (((0,), (0,)), ((), ())), preferred_element_type=F32))
            outs.append(_rms_rows(o_c) * ng * (z_c * jax.nn.sigmoid(z_c)))
        for j, out in enumerate(outs):
            o_ref[0, rows, j * HEAD_DIM:(j + 1) * HEAD_DIM] = out
        return tuple(new_states)

    lax.fori_loop(0, n_chunks, scan,
                  tuple(jnp.zeros((HEAD_DIM, HEAD_DIM), F32) for _ in range(GDN_HEADS)))


def _gated_deltanet_core(proj, gates_raw, conv_w, a_log, dt_bias, norm_g, B, S):
    H = A_HEADS
    C = A_CHUNK
    NH = GDN_HEADS
    HB = H // NH
    assert S % (C * PREP_UNROLL) == 0 and H % NH == 0

    def col(off):
        return pl.BlockSpec((1, S, NH * HEAD_DIM), lambda b, h: (b, 0, off + h))

    def cw(off):
        return pl.BlockSpec((A_CONV, NH * HEAD_DIM), lambda b, h: (0, off + h))

    row128 = pl.BlockSpec((1, LANES), lambda b, h: (0, 0))
    pad = lambda t: jnp.pad(t.reshape(1, -1), ((0, 0), (0, LANES - t.shape[-1])))
    head_buf = pltpu.VMEM((NH, S, HEAD_DIM), F32)
    return pl.pallas_call(
        _gdn_kernel,
        grid=(B, HB),
        in_specs=[col(0), col(HB), col(2 * HB), col(3 * HB), cw(0), cw(HB), cw(2 * HB),
                  pl.BlockSpec((1, S, LANES), lambda b, h: (b, 0, 0)),
                  row128, row128, row128],
        out_specs=pl.BlockSpec((1, S, NH * HEAD_DIM), lambda b, h: (b, 0, h)),
        out_shape=jax.ShapeDtypeStruct((B, S, H * HEAD_DIM), F32),
        scratch_shapes=[head_buf, head_buf,
                        pltpu.VMEM((NH, S, 1), F32), pltpu.VMEM((NH, S, 1), F32),
                        head_buf, head_buf,
                        pltpu.VMEM((NH, S, C), F32),
                        pltpu.VMEM((NH, S // C * SUBLANES, HEAD_DIM), F32)],
        compiler_params=_params(("arbitrary", "arbitrary")),
        name="gated_deltanet",
    )(proj, proj, proj, proj, conv_w, conv_w, conv_w, gates_raw, pad(a_log), pad(dt_bias),
      norm_g.reshape(1, HEAD_DIM))


def _attn_kernel(q_ref, kp_ref, kc_ref, vp_ref, vc_ref, o_ref, lse_ref, *, dil, slopes):
    n = pl.program_id(2)
    T = ATTN_BLOCK
    qi = lax.broadcasted_iota(jnp.int32, (T, 2 * T), 0) + T
    kj = lax.broadcasted_iota(jnp.int32, (T, 2 * T), 1)
    dist = qi - kj
    mask = (dist >= 0) & (dist <= T) & ((kj >= T) | (n > 0))
    distf = (dil * dist).astype(F32)
    lane = lax.broadcasted_iota(jnp.int32, (T, LANES), 1)
    lse_all = jnp.zeros((T, LANES), F32)
    grp = B_HEADS // B_KV_HEADS
    for kvh in range(B_KV_HEADS):
        cs = slice(kvh * HEAD_DIM, (kvh + 1) * HEAD_DIM)
        kcat = jnp.concatenate([kp_ref[0, :, cs], kc_ref[0, :, cs]], axis=0).astype(BF16)
        vcat = jnp.concatenate([vp_ref[0, :, cs], vc_ref[0, :, cs]], axis=0).astype(BF16)
        for gq in range(grp):
            hh = kvh * grp + gq
            hs = slice(hh * HEAD_DIM, (hh + 1) * HEAD_DIM)
            s = _dot_t(q_ref[0, :, hs].astype(BF16), kcat) * (HEAD_DIM ** -0.5)
            s = jnp.where(mask, s - slopes[hh] * distf, -jnp.inf)
            m = jnp.max(s, axis=-1, keepdims=True)
            e = jnp.exp(s - m)
            l = jnp.sum(e, axis=-1, keepdims=True)
            p = e / l
            o_ref[0, :, hs] = jnp.dot(p.astype(BF16), vcat, preferred_element_type=F32)
            lse_all = jnp.where(lane == hh, m + jnp.log(l), lse_all)
    lse_ref[0] = lse_all


def _dilated_group(q, kv, gi, B, S):
    G = len(B_GROUPS)
    window, dil = B_GROUPS[gi]
    assert window // dil == ATTN_BLOCK and S % (dil * ATTN_BLOCK) == 0
    L = S // dil
    nb = L // ATTN_BLOCK
    W = B_HEADS * HEAD_DIM
    KVW = B_KV_HEADS * HEAD_DIM
    n_slopes = G * B_HEADS
    slopes = tuple(2.0 ** (-8.0 * (gi * B_HEADS + hh + 1) / n_slopes) for hh in range(B_HEADS))
    qv = q.reshape(B, L, dil * G * W)
    kvv = kv.reshape(B, L, dil * 2 * G * KVW)
    T = ATTN_BLOCK

    def kspec(off, prev):
        if prev:
            return pl.BlockSpec((1, T, KVW), lambda b, r, n: (b, jnp.maximum(n - 1, 0), r * 2 * G + off))
        return pl.BlockSpec((1, T, KVW), lambda b, r, n: (b, n, r * 2 * G + off))

    o, lse = pl.pallas_call(
        functools.partial(_attn_kernel, dil=dil, slopes=slopes),
        grid=(B, dil, nb),
        in_specs=[pl.BlockSpec((1, T, W), lambda b, r, n: (b, n, r * G + gi)),
                  kspec(gi, True), kspec(gi, False), kspec(G + gi, True), kspec(G + gi, False)],
        out_specs=[pl.BlockSpec((1, T, W), lambda b, r, n: (b, n, r)),
                   pl.BlockSpec((1, T, LANES), lambda b, r, n: (b, n, r))],
        out_shape=[jax.ShapeDtypeStruct((B, L, dil * W), F32),
                   jax.ShapeDtypeStruct((B, L, dil * LANES), F32)],
        compiler_params=_params(("arbitrary", "arbitrary", "arbitrary")),
        name=f"dilated_attn_g{gi}",
    )(qv, kvv, kvv, kvv, kvv)
    return o.reshape(B * S, W), lse.reshape(B * S, LANES)


def _combine_kernel(o0, o1, o2, l0, l1, l2, out_ref):
    ls = [l0[...], l1[...], l2[...]]
    m = jnp.maximum(jnp.maximum(ls[0], ls[1]), ls[2])
    es = [jnp.exp(l - m) for l in ls]
    inv = 1.0 / (es[0] + es[1] + es[2])
    ws = [e * inv for e in es]
    for hh in range(B_HEADS):
        hs = slice(hh * HEAD_DIM, (hh + 1) * HEAD_DIM)
        out_ref[:, hs] = (ws[0][:, hh:hh + 1] * o0[:, hs] + ws[1][:, hh:hh + 1] * o1[:, hs]
                          + ws[2][:, hh:hh + 1] * o2[:, hs])


def _combine_groups(outs, lses, tm=512):
    M, W = outs[0].shape
    tm = _tile(M, tm)
    ospec = pl.BlockSpec((tm, W), lambda i: (i, 0))
    lspec = pl.BlockSpec((tm, LANES), lambda i: (i, 0))
    return pl.pallas_call(
        _combine_kernel,
        grid=(M // tm,),
        in_specs=[ospec] * 3 + [lspec] * 3,
        out_specs=ospec,
        out_shape=jax.ShapeDtypeStruct((M, W), F32),
        compiler_params=_params(("arbitrary",)),
        name="attn_combine",
    )(*outs, *lses)


def _topk_rows(s, pos, k):
    big = float(s.shape[0])
    vals, ids = [], []
    for _ in range(k):
        m = jnp.max(s, axis=0, keepdims=True)
        i = jnp.min(jnp.where(s == m, pos, big), axis=0, keepdims=True)
        vals.append(m)
        ids.append(i)
        s = jnp.where(pos == i, -jnp.inf, s)
    return jnp.concatenate(vals, axis=0), jnp.concatenate(ids, axis=0)


_CAND_PAIRS = [(a, b) for a in range(PEER_TOPK) for b in range(PEER_TOPK)
               if (a + 1) * (b + 1) <= PEER_TOPK]
_N_CAND = len(_CAND_PAIRS)
_CAND_ROWS = -(-_N_CAND // SUBLANES) * SUBLANES


def _cand_select():
    sel = np.zeros((2, _CAND_ROWS, PEER_TOPK), np.float32)
    for r, (a, b) in enumerate(_CAND_PAIRS):
        sel[0, r, a] = 1.0
        sel[1, r, b] = 1.0
    return sel


def _pick_rows(sel, rows):
    hi, mid, lo = _split3(rows)
    return (_dot(sel, hi) + _dot(sel, mid)) + _dot(sel, lo)


def _peer_route_kernel(q_ref, sk_ref, sel_ref, idx_ref, gate_ref):
    T = q_ref.shape[0]
    K = PEER_TOPK
    kpos = lax.broadcasted_iota(jnp.int32, (PEER_KEYS, T), 0).astype(F32)
    cpos_i = lax.broadcasted_iota(jnp.int32, (_CAND_ROWS, T), 0)
    cpos = cpos_i.astype(F32)
    real = cpos_i < _N_CAND
    sk = [sk_ref[0].astype(BF16), sk_ref[1].astype(BF16)]
    sel_a, sel_b = sel_ref[0].astype(BF16), sel_ref[1].astype(BF16)
    idx_rows, gate_rows = [], []
    for hd in range(PEER_HEADS):
        sv, si = [], []
        for p in range(2):
            c0 = (hd * 2 + p) * HEAD_DIM
            sub = _dot_t(sk[p], q_ref[:, c0:c0 + HEAD_DIM].astype(BF16))
            v, i = _topk_rows(sub, kpos, K)
            sv.append(v)
            si.append(i)
        cand_s = jnp.where(real, _pick_rows(sel_a, sv[0]) + _pick_rows(sel_b, sv[1]), -jnp.inf)
        cand_i = _pick_rows(sel_a, si[0]) * float(PEER_KEYS) + _pick_rows(sel_b, si[1])
        top_s, pos = _topk_rows(cand_s, cpos, K)
        e_rows = [jnp.sum(jnp.where(cpos == pos[r:r + 1, :], cand_i, 0.0), axis=0, keepdims=True)
                  for r in range(K)]
        ex = jnp.exp(top_s - top_s[0:1, :])
        gate_rows.append(ex / jnp.sum(ex, axis=0, keepdims=True))
        idx_rows.append(jnp.concatenate(e_rows, axis=0))
    idx_ref[...] = jnp.concatenate(idx_rows, axis=0).T.astype(jnp.int32)
    gate_ref[...] = jnp.concatenate(gate_rows, axis=0).T


def _peer_route(q, subkeys, tt=128):
    M, W = q.shape
    return pl.pallas_call(
        _peer_route_kernel,
        grid=(M // tt,),
        in_specs=[pl.BlockSpec((tt, W), lambda i: (i, 0)),
                  pl.BlockSpec(subkeys.shape, lambda i: (0, 0, 0)),
                  pl.BlockSpec((2, _CAND_ROWS, PEER_TOPK), lambda i: (0, 0, 0))],
        out_specs=[pl.BlockSpec((tt, PEER_SEL), lambda i: (i, 0))] * 2,
        out_shape=[jax.ShapeDtypeStruct((M, PEER_SEL), jnp.int32),
                   jax.ShapeDtypeStruct((M, PEER_SEL), F32)],
        compiler_params=_params(("arbitrary",)),
        name="peer_route",
    )(q, subkeys, jnp.asarray(_cand_select()))


def _peer_expert_kernel(idxc_ref, idxn_ref, gate_ref, x_ref, sh_ref, sc_ref, gf_ref, tab_ref,
                        o_ref, buf, sem, *coef_refs, tb):
    i = pl.program_id(0)
    n = pl.num_programs(0)
    slot = lax.rem(i, 2)
    nslot = 1 - slot
    groups = PEER_SEL // SUBLANES
    half = x_ref.shape[1] // 2
    d_model = x_ref.shape[1] * LANES

    def slab_copy(e, s, r, t):
        return pltpu.make_async_copy(tab_ref.at[e], buf.at[s, r], sem.at[s * tb + t])

    def token_wait(s, t):
        rows = pl.ds(t * PEER_SEL, PEER_SEL)
        pltpu.make_async_copy(buf.at[s, rows], buf.at[s, rows], sem.at[s * tb + t]).wait()

    @pl.when(i == 0)
    def _():
        def body(r8, carry):
            for k in range(SUBLANES):
                r = r8 * SUBLANES + k
                slab_copy(idxc_ref[0, 0, r], 0, r, r8 // groups).start()
            return carry
        lax.fori_loop(0, tb * groups, body, 0)

    x = x_ref[...]
    ms = jnp.sum(jnp.sum(x * x, axis=2, keepdims=True), axis=1, keepdims=True) * (1.0 / d_model)
    hmod = x * lax.rsqrt(ms + NORM_EPS) * (1.0 + sc_ref[...]) + sh_ref[...]
    gf = gf_ref[0]
    ri = lax.broadcasted_iota(jnp.int32, (PEER_SEL, PEER_SEL), 0)
    ci = lax.broadcasted_iota(jnp.int32, (PEER_SEL, PEER_SEL), 1)
    diag = ri == ci
    lane = lax.broadcasted_iota(jnp.int32, (SUBLANES, LANES), 1)
    hi_half = jnp.uint32(0xFFFF0000)
    lag = len(coef_refs) - 1

    acc = None
    for t in range(tb + lag):
        if t % (tb // 2) == 0 and t < tb:
            for tw in range(t, t + tb // 2):
                token_wait(slot, tw)
        if t < tb:
            h0, h1 = hmod[t, :half, :], hmod[t, half:, :]
            dots = jnp.zeros((SUBLANES, LANES), F32)
        if t >= lag:
            acc = [jnp.zeros((half, LANES), F32), jnp.zeros((half, LANES), F32)]
        for g in range(groups):
            if t < tb:
                r0 = t * PEER_SEL + g * SUBLANES
                for k in range(SUBLANES):
                    slab_copy(idxn_ref[0, 0, r0 + k], nslot, r0 + k, t).start(priority=k % 2)
                for m in range(SUBLANES):
                    u = lax.bitcast_convert_type(buf[slot, r0 + m] << 16, F32)
                    part = jnp.sum(u[:half] * h0 + u[half:] * h1, axis=-1, keepdims=True)
                    dots = jnp.where(lane == g * SUBLANES + m, part, dots)
            if t >= lag:
                r0 = (t - lag) * PEER_SEL + g * SUBLANES
                cref = coef_refs[(t - lag) % (lag + 1)]
                prods = [[], []]
                for m in range(SUBLANES):
                    row = g * SUBLANES + m
                    cm = jnp.broadcast_to(cref[row:row + 1, :], (half, LANES))
                    v = lax.bitcast_convert_type(buf[slot, r0 + m] & hi_half, F32)
                    prods[0].append(cm * v[:half])
                    prods[1].append(cm * v[half:])
                for hf in range(2):
                    ps = prods[hf]
                    acc[hf] = acc[hf] + (((ps[0] + ps[1]) + (ps[2] + ps[3]))
                                         + ((ps[4] + ps[5]) + (ps[6] + ps[7])))
        if t < tb:
            act = jnp.sum(dots, axis=0, keepdims=True)
            coef_row = gate_ref[t:t + 1, :] * jax.nn.gelu(act)
            coef_col = jnp.sum(jnp.where(diag, coef_row, 0.0), axis=-1, keepdims=True)
            coef_refs[t % (lag + 1)][...] = jnp.broadcast_to(coef_col, (PEER_SEL, LANES))
        if t >= lag:
            o_ref[t - lag] = x[t - lag] + gf * jnp.concatenate(acc, axis=0)

    @pl.when(i == n - 1)
    def _():
        for t in range(tb):
            token_wait(nslot, t)


def _pack_expert_tables(u, v):
    lo = lax.bitcast_convert_type(u.astype(jnp.bfloat16), jnp.uint16).astype(jnp.uint32)
    hi = lax.bitcast_convert_type(v.astype(jnp.bfloat16), jnp.uint16).astype(jnp.uint32)
    return (lo | (hi << 16)).reshape(u.shape[0], u.shape[1] // LANES, LANES)


def _peer_experts(x, idx, gate, shift, scale, gate_f, table, seq, tb=16):
    M, D = x.shape
    B = shift.shape[0]
    R = D // LANES
    assert R % (2 * SUBLANES) == 0 and R == 2 * SUBLANES
    nblk = M // tb
    bpb = seq // tb
    idx3 = idx.reshape(nblk, 1, tb * PEER_SEL)
    vec = pl.BlockSpec((1, R, LANES), lambda i: (i // bpb, 0, 0))
    out = pl.pallas_call(
        functools.partial(_peer_expert_kernel, tb=tb),
        grid=(nblk,),
        in_specs=[pl.BlockSpec((1, 1, tb * PEER_SEL), lambda i: (i, 0, 0),
                               memory_space=pltpu.SMEM),
                  pl.BlockSpec((1, 1, tb * PEER_SEL), lambda i: (jnp.minimum(i + 1, nblk - 1), 0, 0),
                               memory_space=pltpu.SMEM),
                  pl.BlockSpec((tb, PEER_SEL), lambda i: (i, 0)),
                  pl.BlockSpec((tb, R, LANES), lambda i: (i, 0, 0)),
                  vec, vec, vec,
                  pl.BlockSpec(memory_space=pl.ANY)],
        out_specs=pl.BlockSpec((tb, R, LANES), lambda i: (i, 0, 0)),
        out_shape=jax.ShapeDtypeStruct((M, R, LANES), F32),
        scratch_shapes=[pltpu.VMEM((2, tb * PEER_SEL, R, LANES), jnp.uint32),
                        pltpu.SemaphoreType.DMA((2 * tb,)),
                        ] + [pltpu.VMEM((PEER_SEL, LANES), F32)] * (COEF_LAG + 1),
        compiler_params=_params(("arbitrary",)),
        name="peer_experts",
    )(idx3, idx3, gate, x.reshape(M, R, LANES), shift.reshape(B, R, LANES),
      scale.reshape(B, R, LANES), gate_f.reshape(B, R, LANES), table)
    return out.reshape(M, D)


def _final_kernel(x_ref, g_ref, o_ref):
    o_ref[...] = _rms_rows(x_ref[...]) * g_ref[...]


def _final_norm(x, g, tm=512):
    M, D = x.shape
    tm = _tile(M, tm)
    return pl.pallas_call(
        _final_kernel,
        grid=(M // tm,),
        in_specs=[pl.BlockSpec((tm, D), lambda i: (i, 0)), pl.BlockSpec((1, D), lambda i: (0, 0))],
        out_specs=pl.BlockSpec((tm, D), lambda i: (i, 0)),
        out_shape=jax.ShapeDtypeStruct((M, D), F32),
        compiler_params=_params(("arbitrary",)),
        name="final_norm",
    )(x, g.reshape(1, D))


def kernel(x, c, mod_w, mod_b, a_in_w, a_conv_w, a_log, a_dt_bias, a_norm_g, a_out_w, kv_mod_w,
           kv_mod_b, kv_w, b_q_w, b_out_w, peer_q_w, peer_subkeys, peer_u, peer_v, final_g):
    B, S, D = x.shape
    depth = mod_w.shape[0]
    n_a = a_in_w.shape[0]
    M = B * S
    AW = A_HEADS * HEAD_DIM
    xf = x.reshape(M, D)

    mod = _mod_matmul(c, mod_w, mod_b)
    kv_mod = _mod_matmul(c, kv_mod_w[None], kv_mod_b[None])[0]
    k_v = None
    for layer in range(depth):
        sh_m, sc_m, g_m, sh_f, sc_f, g_f = (mod[layer, :, k * D:(k + 1) * D] for k in range(6))
        if layer < n_a:
            w_in = a_in_w[layer]
            proj = _matmul(xf, w_in, n_out=4 * AW, shift=sh_m, scale=sc_m, seq=S, name="gdn_in_proj")
            w_gate = jnp.pad(w_in[:, 4 * AW:], ((0, 0), (0, LANES - 2 * A_HEADS)))
            gates_raw = _matmul(xf, w_gate, shift=sh_m, scale=sc_m, seq=S, name="gdn_gate_proj")
            o = _gated_deltanet_core(proj.reshape(B, S, 4 * AW), gates_raw.reshape(B, S, LANES),
                                     a_conv_w[layer], a_log[layer], a_dt_bias[layer],
                                     a_norm_g[layer], B, S)
            xf = _matmul(o.reshape(M, AW), a_out_w[layer], res=xf, gate=g_m, seq=S,
                         name="gdn_out_proj")
        else:
            if k_v is None:
                k_v = _matmul(xf, kv_w, shift=kv_mod[:, :D], scale=kv_mod[:, D:], seq=S,
                              name="shared_kv_proj")
            j = layer - n_a
            q = _matmul(xf, b_q_w[j], shift=sh_m, scale=sc_m, seq=S, name="attn_q_proj")
            outs, lses = zip(*[_dilated_group(q, k_v, gi, B, S) for gi in range(len(B_GROUPS))])
            o = _combine_groups(outs, lses)
            xf = _matmul(o, b_out_w[j], res=xf, gate=g_m, seq=S, name="attn_out_proj")
        pq = _matmul(xf, peer_q_w[layer], shift=sh_f, scale=sc_f, seq=S, name="peer_q_proj")
        idx, gate = _peer_route(pq, peer_subkeys[layer])
        table = _pack_expert_tables(peer_u[layer], peer_v[layer])
        xf = _peer_experts(xf, idx, gate, sh_f, sc_f, g_f, table, S)
    return _final_norm(xf, final_g).reshape(B, S, D)
```

```python
import functools

import jax
import jax.numpy as jnp
import numpy as np
from jax import lax
from jax.experimental import pallas as pl
from jax.experimental.pallas import tpu as pltpu

F32 = jnp.float32
BF16 = jnp.bfloat16

NORM_EPS = 1e-6
LANES = 128
SUBLANES = 8

A_HEADS = 16
HEAD_DIM = 128
A_CHUNK = 64
A_CONV = 4
GDN_HEADS = 2
PREP_UNROLL = 4
B_GROUPS = ((128, 1), (512, 4), (2048, 16))
B_HEADS = 16
B_KV_HEADS = 4
ATTN_BLOCK = 128
PEER_HEADS = 8
PEER_KEYS = 128
PEER_TOPK = 16
PEER_SEL = PEER_HEADS * PEER_TOPK
COEF_LAG = 1

VMEM_LIMIT = 56 * 1024 * 1024


def _params(sem):
    return pltpu.CompilerParams(dimension_semantics=sem, vmem_limit_bytes=VMEM_LIMIT)


def _tile(n, pref):
    t = pref
    while t > LANES and n % t:
        t //= 2
    assert n % t == 0, (n, pref)
    return t


def _rms_rows(x):
    return x * lax.rsqrt(jnp.mean(x * x, axis=-1, keepdims=True) + NORM_EPS)


def _dot(a, b):
    return jnp.dot(a, b, preferred_element_type=F32)


def _dot_t(a, b):
    return lax.dot_general(a, b, (((1,), (1,)), ((), ())), preferred_element_type=F32)


def _split2(a):
    hi = a.astype(BF16)
    return hi, (a - hi.astype(F32)).astype(BF16)


def _split3(a):
    hi = a.astype(BF16)
    r = a - hi.astype(F32)
    mid = r.astype(BF16)
    return hi, mid, (r - mid.astype(F32)).astype(BF16)


def _dot3(a2, b2):
    (ah, al), (bh, bl) = a2, b2
    return _dot(ah, bh) + (_dot(ah, bl) + _dot(al, bh))


def _mod_kernel(c_ref, w_ref, b_ref, o_ref):
    c = c_ref[...]
    cs = (c * jax.nn.sigmoid(c)).astype(BF16)
    o_ref[0] = jnp.dot(cs, w_ref[0].astype(BF16), preferred_element_type=F32) + b_ref[0]


def _mod_matmul(c, w, b, tn=1024):
    L, D, N = w.shape
    B = c.shape[0]
    tn = _tile(N, tn)
    return pl.pallas_call(
        _mod_kernel,
        grid=(L, N // tn),
        in_specs=[pl.BlockSpec((B, D), lambda l, j: (0, 0)),
                  pl.BlockSpec((1, D, tn), lambda l, j: (l, 0, j)),
                  pl.BlockSpec((1, 1, tn), lambda l, j: (l, 0, j))],
        out_specs=pl.BlockSpec((1, B, tn), lambda l, j: (l, 0, j)),
        out_shape=jax.ShapeDtypeStruct((L, B, N), F32),
        compiler_params=_params(("arbitrary", "arbitrary")),
        name="adaln_mod",
    )(c, w, b.reshape(L, 1, N))


def _mm_kernel(*refs, modulate, residual):
    it = iter(refs)
    a_ref = next(it)
    if modulate:
        sh_ref, sc_ref = next(it), next(it)
    w_ref = next(it)
    if residual:
        res_ref, g_ref = next(it), next(it)
    o_ref = next(it)
    a_bf = next(it)

    @pl.when(pl.program_id(1) == 0)
    def _():
        a = a_ref[...]
        if modulate:
            a = _rms_rows(a) * (1.0 + sc_ref[0]) + sh_ref[0]
        a_bf[...] = a.astype(BF16)

    acc = jnp.dot(a_bf[...], w_ref[...].astype(BF16), preferred_element_type=F32)
    if residual:
        acc = res_ref[...] + g_ref[0] * acc
    o_ref[...] = acc


def _matmul(a, w, *, n_out=None, shift=None, scale=None, res=None, gate=None, seq=None,
            tm=1024, tn=1024, name="matmul"):
    M, K = a.shape
    N = w.shape[1] if n_out is None else n_out
    tn = _tile(N, tn)
    tm = _tile(seq, tm)
    modulate = shift is not None
    residual = res is not None
    bpb = seq // tm
    args = [a]
    specs = [pl.BlockSpec((tm, K), lambda i, j: (i, 0))]
    if modulate:
        B = shift.shape[0]
        args += [shift.reshape(B, 1, K), scale.reshape(B, 1, K)]
        specs += [pl.BlockSpec((1, 1, K), lambda i, j: (i // bpb, 0, 0))] * 2
    args.append(w.astype(BF16))
    specs.append(pl.BlockSpec((K, tn), lambda i, j: (0, j)))
    if residual:
        B = gate.shape[0]
        args += [res, gate.reshape(B, 1, N)]
        specs += [pl.BlockSpec((tm, tn), lambda i, j: (i, j)),
                  pl.BlockSpec((1, 1, tn), lambda i, j: (i // bpb, 0, j))]
    return pl.pallas_call(
        functools.partial(_mm_kernel, modulate=modulate, residual=residual),
        grid=(M // tm, N // tn),
        in_specs=specs,
        out_specs=pl.BlockSpec((tm, tn), lambda i, j: (i, j)),
        out_shape=jax.ShapeDtypeStruct((M, N), F32),
        scratch_shapes=[pltpu.VMEM((tm, K), BF16)],
        compiler_params=_params(("arbitrary", "arbitrary")),
        name=name,
    )(*args)


def _gdn_kernel(q_ref, k_ref, v_ref, z_ref, cwq_ref, cwk_ref, cwv_ref, gr_ref, alog_ref, dtb_ref,
                ng_ref, o_ref, q_s, k_s, g_s, beta_s, u_s, w_s, qk_s, cd_s):
    hp = pl.program_id(1)
    S = q_ref.shape[1]
    C = A_CHUNK
    n_chunks = S // C
    row = lax.broadcasted_iota(jnp.int32, (S, HEAD_DIM), 0)
    lane = lax.broadcasted_iota(jnp.int32, (1, LANES), 1)

    def conv_silu(x, w):
        y = x * w[A_CONV - 1:A_CONV, :]
        for back in range(1, A_CONV):
            xs = jnp.where(row >= back, pltpu.roll(x, back, axis=0), 0.0)
            y = y + xs * w[A_CONV - 1 - back:A_CONV - back, :]
        return y * jax.nn.sigmoid(y)

    def l2n(t):
        return t * lax.rsqrt(jnp.sum(t * t, axis=-1, keepdims=True) + NORM_EPS)

    gr = gr_ref[0]
    for j in range(GDN_HEADS):
        hs = slice(j * HEAD_DIM, (j + 1) * HEAD_DIM)
        h = hp * GDN_HEADS + j
        q_s[j] = l2n(conv_silu(q_ref[0, :, hs], cwq_ref[:, hs])) * (HEAD_DIM ** -0.5)
        k_s[j] = l2n(conv_silu(k_ref[0, :, hs], cwk_ref[:, hs]))
        vv = conv_silu(v_ref[0, :, hs], cwv_ref[:, hs])
        oh_b = (lane == h).astype(F32)
        oh_a = (lane == h + A_HEADS).astype(F32)
        b_raw = jnp.sum(gr * oh_b, axis=-1, keepdims=True)
        a_raw = jnp.sum(gr * oh_a, axis=-1, keepdims=True)
        a_log = jnp.sum(alog_ref[...] * oh_b, axis=-1, keepdims=True)
        dtb = jnp.sum(dtb_ref[...] * oh_b, axis=-1, keepdims=True)
        sp_in = a_raw + dtb
        softplus = jnp.maximum(sp_in, 0.0) + jnp.log1p(jnp.exp(-jnp.abs(sp_in)))
        g_s[j] = -jnp.exp(a_log) * softplus
        beta = jax.nn.sigmoid(b_raw)
        beta_s[j] = beta
        u_s[j] = vv * beta

    ri = lax.broadcasted_iota(jnp.int32, (C, C), 0)
    ci = lax.broadcasted_iota(jnp.int32, (C, C), 1)
    incl = ri >= ci
    strict = ri > ci
    lower = incl.astype(BF16)
    upper = (ri <= ci).astype(F32)
    ones = jnp.ones((C, C), BF16)

    def chunk_rows(c):
        return pl.ds(pl.multiple_of(c * C, C), C)

    def cd_rows(c, n):
        return pl.ds(pl.multiple_of(c * SUBLANES, SUBLANES), n)

    def prep_load(j, c):
        rows = chunk_rows(c)
        return g_s[j, rows, :], beta_s[j, rows, :], k_s[j, rows, :], q_s[j, rows, :], u_s[j, rows, :]

    def prep_compute(loaded):
        st = []
        for g_c, b_c, k_c, q_c, v_c in loaded:
            gb_hi, gb_lo = _split2(jnp.broadcast_to(g_c, (C, C)))
            gu_hi, gu_lo = _split2(g_c * upper)
            gcol = _dot(lower, gb_hi) + _dot(lower, gb_lo)
            grow = _dot(ones, gu_hi) + _dot(ones, gu_lo)
            k_bf = k_c.astype(BF16)
            st.append(dict(gcol=gcol, grow=grow, k_bf=k_bf, kk=_dot_t(k_bf, k_bf),
                           qk=_dot_t(q_c.astype(BF16), k_bf)))
        for s, (g_c, b_c, k_c, q_c, v_c) in zip(st, loaded):
            decay = jnp.exp(jnp.where(incl, s["gcol"] - s["grow"], -jnp.inf))
            gam = s["gcol"][:, 0:1]
            gam_last = s["grow"][:, C - 1:C]
            egam = jnp.exp(gam)
            a_mat = jnp.where(strict, s["kk"] * decay * b_c, 0.0)
            rhs = jnp.concatenate([v_c, k_c * (b_c * egam)], axis=1)
            s.update(qk=s["qk"] * decay, qd=q_c * egam, kd=k_c * jnp.exp(gam_last - gam),
                     cd=jnp.broadcast_to(jnp.exp(gam_last[0:SUBLANES, :]), (SUBLANES, HEAD_DIM)),
                     pw=_split2(a_mat), rhs=rhs)
        for s in st:
            s["x"] = s["rhs"] - _dot3(s["pw"], _split2(s["rhs"]))
        for _ in range(5):
            for s in st:
                s["pw"] = _split2(_dot3(s["pw"], s["pw"]))
            for s in st:
                s["x"] = s["x"] + _dot3(s["pw"], _split2(s["x"]))
        return [(s["x"][:, :HEAD_DIM], s["x"][:, HEAD_DIM:], s["qk"], s["qd"], s["kd"], s["cd"])
                for s in st]

    def prep_store(j, c, u, w, qk, qd, kd, cd):
        rows = chunk_rows(c)
        u_s[j, rows, :] = u
        w_s[j, rows, :] = w
        qk_s[j, rows, :] = qk
        q_s[j, rows, :] = qd
        k_s[j, rows, :] = kd
        cd_s[j, cd_rows(c, SUBLANES), :] = cd

    def prep(i, carry):
        work = [(j, i * PREP_UNROLL + k) for j in range(GDN_HEADS) for k in range(PREP_UNROLL)]
        loaded = [prep_load(j, c) for j, c in work]
        outs = prep_compute(loaded)
        for (j, c), out in zip(work, outs):
            prep_store(j, c, *out)
        return carry

    lax.fori_loop(0, n_chunks // PREP_UNROLL, prep, 0)

    ng = ng_ref[...]

    def scan(c, states):
        rows = chunk_rows(c)
        loaded = [(u_s[j, rows, :], w_s[j, rows, :], q_s[j, rows, :], qk_s[j, rows, :],
                   k_s[j, rows, :], cd_s[j, cd_rows(c, 1), :], z_ref[0, rows, j * HEAD_DIM:(j + 1) * HEAD_DIM])
                  for j in range(GDN_HEADS)]
        new_states, outs = [], []
        for state, (u_c, w_c, qd_c, qk_c, kd_c, cd, z_c) in zip(states, loaded):
            st_bf = state.astype(BF16)
            v_new = u_c - _dot(w_c.astype(BF16), st_bf)
            vn_bf = v_new.astype(BF16)
            o_c = _dot(qd_c.astype(BF16), st_bf) + _dot(qk_c.astype(BF16), vn_bf)
            new_states.append(state * cd + lax.dot_general(
                kd_c.astype(BF16), vn_bf, (((0,), (0,)), ((), ())), preferred_element_type=F32))
            outs.append(_rms_rows(o_c) * ng * (z_c * jax.nn.sigmoid(z_c)))
        for j, out in enumerate(outs):
            o_ref[0, rows, j * HEAD_DIM:(j + 1) * HEAD_DIM] = out
        return tuple(new_states)

    lax.fori_loop(0, n_chunks, scan,
                  tuple(jnp.zeros((HEAD_DIM, HEAD_DIM), F32) for _ in range(GDN_HEADS)))


def _gated_deltanet_core(proj, gates_raw, conv_w, a_log, dt_bias, norm_g, B, S):
    H = A_HEADS
    C = A_CHUNK
    NH = GDN_HEADS
    HB = H // NH
    assert S % (C * PREP_UNROLL) == 0 and H % NH == 0

    def col(off):
        return pl.BlockSpec((1, S, NH * HEAD_DIM), lambda b, h: (b, 0, off + h))

    def cw(off):
        return pl.BlockSpec((A_CONV, NH * HEAD_DIM), lambda b, h: (0, off + h))

    row128 = pl.BlockSpec((1, LANES), lambda b, h: (0, 0))
    pad = lambda t: jnp.pad(t.reshape(1, -1), ((0, 0), (0, LANES - t.shape[-1])))
    head_buf = pltpu.VMEM((NH, S, HEAD_DIM), F32)
    return pl.pallas_call(
        _gdn_kernel,
        grid=(B, HB),
        in_specs=[col(0), col(HB), col(2 * HB), col(3 * HB), cw(0), cw(HB), cw(2 * HB),
                  pl.BlockSpec((1, S, LANES), lambda b, h: (b, 0, 0)),
                  row128, row128, row128],
        out_specs=pl.BlockSpec((1, S, NH * HEAD_DIM), lambda b, h: (b, 0, h)),
        out_shape=jax.ShapeDtypeStruct((B, S, H * HEAD_DIM), F32),
        scratch_shapes=[head_buf, head_buf,
                        pltpu.VMEM((NH, S, 1), F32), pltpu.VMEM((NH, S, 1), F32),
                        head_buf, head_buf,
                        pltpu.VMEM((NH, S, C), F32),
                        pltpu.VMEM((NH, S // C * SUBLANES, HEAD_DIM), F32)],
        compiler_params=_params(("arbitrary", "arbitrary")),
        name="gated_deltanet",
    )(proj, proj, proj, proj, conv_w, conv_w, conv_w, gates_raw, pad(a_log), pad(dt_bias),
      norm_g.reshape(1, HEAD_DIM))


def _attn_kernel(q_ref, kp_ref, kc_ref, vp_ref, vc_ref, o_ref, lse_ref, *, dil, slopes):
    n = pl.program_id(2)
    T = ATTN_BLOCK
    qi = lax.broadcasted_iota(jnp.int32, (T, 2 * T), 0) + T
    kj = lax.broadcasted_iota(jnp.int32, (T, 2 * T), 1)
    dist = qi - kj
    mask = (dist >= 0) & (dist <= T) & ((kj >= T) | (n > 0))
    distf = (dil * dist).astype(F32)
    lane = lax.broadcasted_iota(jnp.int32, (T, LANES), 1)
    lse_all = jnp.zeros((T, LANES), F32)
    grp = B_HEADS // B_KV_HEADS
    for kvh in range(B_KV_HEADS):
        cs = slice(kvh * HEAD_DIM, (kvh + 1) * HEAD_DIM)
        kcat = jnp.concatenate([kp_ref[0, :, cs], kc_ref[0, :, cs]], axis=0).astype(BF16)
        vcat = jnp.concatenate([vp_ref[0, :, cs], vc_ref[0, :, cs]], axis=0).astype(BF16)
        for gq in range(grp):
            hh = kvh * grp + gq
            hs = slice(hh * HEAD_DIM, (hh + 1) * HEAD_DIM)
            s = _dot_t(q_ref[0, :, hs].astype(BF16), kcat) * (HEAD_DIM ** -0.5)
            s = jnp.where(mask, s - slopes[hh] * distf, -jnp.inf)
            m = jnp.max(s, axis=-1, keepdims=True)
            e = jnp.exp(s - m)
            l = jnp.sum(e, axis=-1, keepdims=True)
            p = e / l
            o_ref[0, :, hs] = jnp.dot(p.astype(BF16), vcat, preferred_element_type=F32)
            lse_all = jnp.where(lane == hh, m + jnp.log(l), lse_all)
    lse_ref[0] = lse_all


def _dilated_group(q, kv, gi, B, S):
    G = len(B_GROUPS)
    window, dil = B_GROUPS[gi]
    assert window // dil == ATTN_BLOCK and S % (dil * ATTN_BLOCK) == 0
    L = S // dil
    nb = L // ATTN_BLOCK
    W = B_HEADS * HEAD_DIM
    KVW = B_KV_HEADS * HEAD_DIM
    n_slopes = G * B_HEADS
    slopes = tuple(2.0 ** (-8.0 * (gi * B_HEADS + hh + 1) / n_slopes) for hh in range(B_HEADS))
    qv = q.reshape(B, L, dil * G * W)
    kvv = kv.reshape(B, L, dil * 2 * G * KVW)
    T = ATTN_BLOCK

    def kspec(off, prev):
        if prev:
            return pl.BlockSpec((1, T, KVW), lambda b, r, n: (b, jnp.maximum(n - 1, 0), r * 2 * G + off))
        return pl.BlockSpec((1, T, KVW), lambda b, r, n: (b, n, r * 2 * G + off))

    o, lse = pl.pallas_call(
        functools.partial(_attn_kernel, dil=dil, slopes=slopes),
        grid=(B, dil, nb),
        in_specs=[pl.BlockSpec((1, T, W), lambda b, r, n: (b, n, r * G + gi)),
                  kspec(gi, True), kspec(gi, False), kspec(G + gi, True), kspec(G + gi, False)],
        out_specs=[pl.BlockSpec((1, T, W), lambda b, r, n: (b, n, r)),
                   pl.BlockSpec((1, T, LANES), lambda b, r, n: (b, n, r))],
        out_shape=[jax.ShapeDtypeStruct((B, L, dil * W), F32),
                   jax.ShapeDtypeStruct((B, L, dil * LANES), F32)],
        compiler_params=_params(("arbitrary", "arbitrary", "arbitrary")),
        name=f"dilated_attn_g{gi}",
    )(qv, kvv, kvv, kvv, kvv)
    return o.reshape(B * S, W), lse.reshape(B * S, LANES)


def _combine_kernel(o0, o1, o2, l0, l1, l2, out_ref):
    ls = [l0[...], l1[...], l2[...]]
    m = jnp.maximum(jnp.maximum(ls[0], ls[1]), ls[2])
    es = [jnp.exp(l - m) for l in ls]
    inv = 1.0 / (es[0] + es[1] + es[2])
    ws = [e * inv for e in es]
    for hh in range(B_HEADS):
        hs = slice(hh * HEAD_DIM, (hh + 1) * HEAD_DIM)
        out_ref[:, hs] = (ws[0][:, hh:hh + 1] * o0[:, hs] + ws[1][:, hh:hh + 1] * o1[:, hs]
                          + ws[2][:, hh:hh + 1] * o2[:, hs])


def _combine_groups(outs, lses, tm=512):
    M, W = outs[0].shape
    tm = _tile(M, tm)
    ospec = pl.BlockSpec((tm, W), lambda i: (i, 0))
    lspec = pl.BlockSpec((tm, LANES), lambda i: (i, 0))
    return pl.pallas_call(
        _combine_kernel,
        grid=(M // tm,),
        in_specs=[ospec] * 3 + [lspec] * 3,
        out_specs=ospec,
        out_shape=jax.ShapeDtypeStruct((M, W), F32),
        compiler_params=_params(("arbitrary",)),
        name="attn_combine",
    )(*outs, *lses)


def _topk_rows(s, pos, k):
    big = float(s.shape[0])
    vals, ids = [], []
    for _ in range(k):
        m = jnp.max(s, axis=0, keepdims=True)
        i = jnp.min(jnp.where(s == m, pos, big), axis=0, keepdims=True)
        vals.append(m)
        ids.append(i)
        s = jnp.where(pos == i, -jnp.inf, s)
    return jnp.concatenate(vals, axis=0), jnp.concatenate(ids, axis=0)


_CAND_PAIRS = [(a, b) for a in range(PEER_TOPK) for b in range(PEER_TOPK)
               if (a + 1) * (b + 1) <= PEER_TOPK]
_N_CAND = len(_CAND_PAIRS)
_CAND_ROWS = -(-_N_CAND // SUBLANES) * SUBLANES


def _cand_select():
    sel = np.zeros((2, _CAND_ROWS, PEER_TOPK), np.float32)
    for r, (a, b) in enumerate(_CAND_PAIRS):
        sel[0, r, a] = 1.0
        sel[1, r, b] = 1.0
    return sel


def _pick_rows(sel, rows):
    hi, mid, lo = _split3(rows)
    return (_dot(sel, hi) + _dot(sel, mid)) + _dot(sel, lo)


def _peer_route_kernel(q_ref, sk_ref, sel_ref, idx_ref, gate_ref):
    T = q_ref.shape[0]
    K = PEER_TOPK
    kpos = lax.broadcasted_iota(jnp.int32, (PEER_KEYS, T), 0).astype(F32)
    cpos_i = lax.broadcasted_iota(jnp.int32, (_CAND_ROWS, T), 0)
    cpos = cpos_i.astype(F32)
    real = cpos_i < _N_CAND
    sk = [sk_ref[0].astype(BF16), sk_ref[1].astype(BF16)]
    sel_a, sel_b = sel_ref[0].astype(BF16), sel_ref[1].astype(BF16)
    idx_rows, gate_rows = [], []
    for hd in range(PEER_HEADS):
        sv, si = [], []
        for p in range(2):
            c0 = (hd * 2 + p) * HEAD_DIM
            sub = _dot_t(sk[p], q_ref[:, c0:c0 + HEAD_DIM].astype(BF16))
            v, i = _topk_rows(sub, kpos, K)
            sv.append(v)
            si.append(i)
        cand_s = jnp.where(real, _pick_rows(sel_a, sv[0]) + _pick_rows(sel_b, sv[1]), -jnp.inf)
        cand_i = _pick_rows(sel_a, si[0]) * float(PEER_KEYS) + _pick_rows(sel_b, si[1])
        top_s, pos = _topk_rows(cand_s, cpos, K)
        e_rows = [jnp.sum(jnp.where(cpos == pos[r:r + 1, :], cand_i, 0.0), axis=0, keepdims=True)
                  for r in range(K)]
        ex = jnp.exp(top_s - top_s[0:1, :])
        gate_rows.append(ex / jnp.sum(ex, axis=0, keepdims=True))
        idx_rows.append(jnp.concatenate(e_rows, axis=0))
    idx_ref[...] = jnp.concatenate(idx_rows, axis=0).T.astype(jnp.int32)
    gate_ref[...] = jnp.concatenate(gate_rows, axis=0).T


def _peer_route(q, subkeys, tt=256):
    M, W = q.shape
    return pl.pallas_call(
        _peer_route_kernel,
        grid=(M // tt,),
        in_specs=[pl.BlockSpec((tt, W), lambda i: (i, 0)),
                  pl.BlockSpec(subkeys.shape, lambda i: (0, 0, 0)),
                  pl.BlockSpec((2, _CAND_ROWS, PEER_TOPK), lambda i: (0, 0, 0))],
        out_specs=[pl.BlockSpec((tt, PEER_SEL), lambda i: (i, 0))] * 2,
        out_shape=[jax.ShapeDtypeStruct((M, PEER_SEL), jnp.int32),
                   jax.ShapeDtypeStruct((M, PEER_SEL), F32)],
        compiler_params=_params(("arbitrary",)),
        name="peer_route",
    )(q, subkeys, jnp.asarray(_cand_select()))


def _peer_expert_kernel(idxc_ref, idxn_ref, gate_ref, x_ref, sh_ref, sc_ref, gf_ref, tab_ref,
                        o_ref, buf, sem, *coef_refs, tb):
    i = pl.program_id(0)
    n = pl.num_programs(0)
    slot = lax.rem(i, 2)
    nslot = 1 - slot
    groups = PEER_SEL // SUBLANES
    half = x_ref.shape[1] // 2
    d_model = x_ref.shape[1] * LANES

    def slab_copy(e, s, r, t):
        return pltpu.make_async_copy(tab_ref.at[e], buf.at[s, r], sem.at[s * tb + t])

    def token_wait(s, t):
        rows = pl.ds(t * PEER_SEL, PEER_SEL)
        pltpu.make_async_copy(buf.at[s, rows], buf.at[s, rows], sem.at[s * tb + t]).wait()

    @pl.when(i == 0)
    def _():
        def body(r8, carry):
            for k in range(SUBLANES):
                r = r8 * SUBLANES + k
                slab_copy(idxc_ref[0, 0, r], 0, r, r8 // groups).start()
            return carry
        lax.fori_loop(0, tb * groups, body, 0)

    x = x_ref[...]
    ms = jnp.sum(jnp.sum(x * x, axis=2, keepdims=True), axis=1, keepdims=True) * (1.0 / d_model)
    hmod = x * lax.rsqrt(ms + NORM_EPS) * (1.0 + sc_ref[...]) + sh_ref[...]
    gf = gf_ref[0]
    ri = lax.broadcasted_iota(jnp.int32, (PEER_SEL, PEER_SEL), 0)
    ci = lax.broadcasted_iota(jnp.int32, (PEER_SEL, PEER_SEL), 1)
    diag = ri == ci
    lane = lax.broadcasted_iota(jnp.int32, (SUBLANES, LANES), 1)
    hi_half = jnp.uint32(0xFFFF0000)
    lag = len(coef_refs) - 1

    acc = None
    for t in range(tb + lag):
        if t % (tb // 2) == 0 and t < tb:
            for tw in range(t, t + tb // 2):
                token_wait(slot, tw)
        if t < tb:
            h0, h1 = hmod[t, :half, :], hmod[t, half:, :]
            dots = jnp.zeros((SUBLANES, LANES), F32)
        if t >= lag:
            acc = [jnp.zeros((half, LANES), F32), jnp.zeros((half, LANES), F32)]
        for g in range(groups):
            if t < tb:
                r0 = t * PEER_SEL + g * SUBLANES
                for k in range(SUBLANES):
                    slab_copy(idxn_ref[0, 0, r0 + k], nslot, r0 + k, t).start(priority=k % 2)
                for m in range(SUBLANES):
                    u = lax.bitcast_convert_type(buf[slot, r0 + m] << 16, F32)
                    part = jnp.sum(u[:half] * h0 + u[half:] * h1, axis=-1, keepdims=True)
                    dots = jnp.where(lane == g * SUBLANES + m, part, dots)
            if t >= lag:
                r0 = (t - lag) * PEER_SEL + g * SUBLANES
                cref = coef_refs[(t - lag) % (lag + 1)]
                prods = [[], []]
                for m in range(SUBLANES):
                    row = g * SUBLANES + m
                    cm = jnp.broadcast_to(cref[row:row + 1, :], (half, LANES))
                    v = lax.bitcast_convert_type(buf[slot, r0 + m] & hi_half, F32)
                    prods[0].append(cm * v[:half])
                    prods[1].append(cm * v[half:])
                for hf in range(2):
                    ps = prods[hf]
                    acc[hf] = acc[hf] + (((ps[0] + ps[1]) + (ps[2] + ps[3]))
                                         + ((ps[4] + ps[5]) + (ps[6] + ps[7])))
        if t < tb:
            act = jnp.sum(dots, axis=0, keepdims=True)
            coef_row = gate_ref[t:t + 1, :] * jax.nn.gelu(act)
            coef_col = jnp.sum(jnp.where(diag, coef_row, 0.0), axis=-1, keepdims=True)
            coef_refs[t % (lag + 1)][...] = jnp.broadcast_to(coef_col, (PEER_SEL, LANES))
        if t >= lag:
            o_ref[t - lag] = x[t - lag] + gf * jnp.concatenate(acc, axis=0)

    @pl.when(i == n - 1)
    def _():
        for t in range(tb):
            token_wait(nslot, t)


def _bf16_bits(x):
    b = lax.bitcast_convert_type(x, jnp.uint32)
    r = (b + jnp.uint32(0x7FFF) + ((b >> 16) & jnp.uint32(1))) >> 16
    return jnp.where(x != x, jnp.uint32(0x7FC0), r)


def _pack_expert_tables(u, v):
    return (_bf16_bits(u) | (_bf16_bits(v) << 16)).reshape(u.shape[0], u.shape[1] // LANES, LANES)


def _peer_experts(x, idx, gate, shift, scale, gate_f, table, seq, tb=16):
    M, D = x.shape
    B = shift.shape[0]
    R = D // LANES
    assert R % (2 * SUBLANES) == 0 and R == 2 * SUBLANES
    nblk = M // tb
    bpb = seq // tb
    idx3 = idx.reshape(nblk, 1, tb * PEER_SEL)
    vec = pl.BlockSpec((1, R, LANES), lambda i: (i // bpb, 0, 0))
    out = pl.pallas_call(
        functools.partial(_peer_expert_kernel, tb=tb),
        grid=(nblk,),
        in_specs=[pl.BlockSpec((1, 1, tb * PEER_SEL), lambda i: (i, 0, 0),
                               memory_space=pltpu.SMEM),
                  pl.BlockSpec((1, 1, tb * PEER_SEL), lambda i: (jnp.minimum(i + 1, nblk - 1), 0, 0),
                               memory_space=pltpu.SMEM),
                  pl.BlockSpec((tb, PEER_SEL), lambda i: (i, 0)),
                  pl.BlockSpec((tb, R, LANES), lambda i: (i, 0, 0)),
                  vec, vec, vec,
                  pl.BlockSpec(memory_space=pl.ANY)],
        out_specs=pl.BlockSpec((tb, R, LANES), lambda i: (i, 0, 0)),
        out_shape=jax.ShapeDtypeStruct((M, R, LANES), F32),
        scratch_shapes=[pltpu.VMEM((2, tb * PEER_SEL, R, LANES), jnp.uint32),
                        pltpu.SemaphoreType.DMA((2 * tb,)),
                        ] + [pltpu.VMEM((PEER_SEL, LANES), F32)] * (COEF_LAG + 1),
        compiler_params=_params(("arbitrary",)),
        name="peer_experts",
    )(idx3, idx3, gate, x.reshape(M, R, LANES), shift.reshape(B, R, LANES),
      scale.reshape(B, R, LANES), gate_f.reshape(B, R, LANES), table)
    return out.reshape(M, D)


def _final_kernel(x_ref, g_ref, o_ref):
    o_ref[...] = _rms_rows(x_ref[...]) * g_ref[...]


def _final_norm(x, g, tm=512):
    M, D = x.shape
    tm = _tile(M, tm)
    return pl.pallas_call(
        _final_kernel,
        grid=(M // tm,),
        in_specs=[pl.BlockSpec((tm, D), lambda i: (i, 0)), pl.BlockSpec((1, D), lambda i: (0, 0))],
        out_specs=pl.BlockSpec((tm, D), lambda i: (i, 0)),
        out_shape=jax.ShapeDtypeStruct((M, D), F32),
        compiler_params=_params(("arbitrary",)),
        name="final_norm",
    )(x, g.reshape(1, D))


def kernel(x, c, mod_w, mod_b, a_in_w, a_conv_w, a_log, a_dt_bias, a_norm_g, a_out_w, kv_mod_w,
           kv_mod_b, kv_w, b_q_w, b_out_w, peer_q_w, peer_subkeys, peer_u, peer_v, final_g):
    B, S, D = x.shape
    depth = mod_w.shape[0]
    n_a = a_in_w.shape[0]
    M = B * S
    AW = A_HEADS * HEAD_DIM
    xf = x.reshape(M, D)

    mod = _mod_matmul(c, mod_w, mod_b)
    kv_mod = _mod_matmul(c, kv_mod_w[None], kv_mod_b[None])[0]
    k_v = None
    for layer in range(depth):
        sh_m, sc_m, g_m, sh_f, sc_f, g_f = (mod[layer, :, k * D:(k + 1) * D] for k in range(6))
        if layer < n_a:
            w_in = a_in_w[layer]
            proj = _matmul(xf, w_in, n_out=4 * AW, shift=sh_m, scale=sc_m, seq=S, name="gdn_in_proj")
            w_gate = jnp.pad(w_in[:, 4 * AW:], ((0, 0), (0, LANES - 2 * A_HEADS)))
            gates_raw = _matmul(xf, w_gate, shift=sh_m, scale=sc_m, seq=S, name="gdn_gate_proj")
            o = _gated_deltanet_core(proj.reshape(B, S, 4 * AW), gates_raw.reshape(B, S, LANES),
                                     a_conv_w[layer], a_log[layer], a_dt_bias[layer],
                                     a_norm_g[layer], B, S)
            xf = _matmul(o.reshape(M, AW), a_out_w[layer], res=xf, gate=g_m, seq=S,
                         name="gdn_out_proj")
        else:
            if k_v is None:
                k_v = _matmul(xf, kv_w, shift=kv_mod[:, :D], scale=kv_mod[:, D:], seq=S,
                              name="shared_kv_proj")
            j = layer - n_a
            q = _matmul(xf, b_q_w[j], shift=sh_m, scale=sc_m, seq=S, name="attn_q_proj")
            outs, lses = zip(*[_dilated_group(q, k_v, gi, B, S) for gi in range(len(B_GROUPS))])
            o = _combine_groups(outs, lses)
            xf = _matmul(o, b_out_w[j], res=xf, gate=g_m, seq=S, name="attn_out_proj")
        pq = _matmul(xf, peer_q_w[layer], shift=sh_f, scale=sc_f, seq=S, name="peer_q_proj")
        idx, gate = _peer_route(pq, peer_subkeys[layer])
        table = _pack_expert_tables(peer_u[layer], peer_v[layer])
        xf = _peer_experts(xf, idx, gate, sh_f, sc_f, g_f, table, S)
    return _final_norm(xf, final_g).reshape(B, S, D)
```
